```python
import math
import jax, jax.numpy as jnp
from jax import lax
import numpy as np

D_MODEL = 2048
BATCH = 2
SEQ = 4096
DEPTH = 4
DEC_BATCH = 8
DEC_SEQ = 8
PAST_LEN = 16384
PAGE_SIZE = 128

ROPE_THETA = 10000.0
NORM_EPS = 1e-6
H_A = 8
D_A = 64
W_A = H_A * 2 * D_A
H_B = 8
D_B = 128
W_B = H_B * D_B
MOBA_BLOCK = 256
MOBA_TOPK = 3
Q_CHUNK = 128

kernel_name = 'hybrid_diffattn_moba_decode_step'


def rms_norm(x, g):
    xf = x.astype(jnp.float32)
    y = xf * lax.rsqrt(jnp.mean(xf * xf, axis=-1, keepdims=True) + NORM_EPS)
    return (y * g.astype(jnp.float32)).astype(x.dtype)


def rope(x, pos):
    d = x.shape[-1]
    half = d // 2
    inv = ROPE_THETA ** (-2.0 * jnp.arange(half, dtype=jnp.float32) / d)
    ang = pos.astype(jnp.float32)[:, None] * inv[None, :]
    cos = jnp.cos(ang)[None, :, None, :]
    sin = jnp.sin(ang)[None, :, None, :]
    xf = x.astype(jnp.float32)
    x1, x2 = xf[..., :half], xf[..., half:]
    return jnp.concatenate([x1 * cos - x2 * sin, x2 * cos + x1 * sin], axis=-1).astype(x.dtype)


def lambda_init_of(layer):
    return 0.8 - 0.6 * math.exp(-0.3 * layer)


def diff_attention(q1, q2, k1, k2, v, lam, q_start):
    B, Tq, H, d = q1.shape
    L = k1.shape[1]
    qcs = math.gcd(Tq, Q_CHUNK)
    nc = Tq // qcs
    scale = d ** -0.5
    kpos = jnp.arange(L)

    def to_chunks(q):
        return q.reshape(B, nc, qcs, H, d).transpose(1, 0, 2, 3, 4)

    def one_chunk(args):
        q1c, q2c, c = args
        qpos = q_start + c * qcs + jnp.arange(qcs)
        mask = kpos[None, :] <= qpos[:, None]

        def probs(qq, kk):
            s = jnp.einsum('bqhd,bkhd->bhqk', qq, kk, preferred_element_type=jnp.float32) * scale
            return jax.nn.softmax(jnp.where(mask, s, -jnp.inf), axis=-1)

        p = probs(q1c, k1) - lam * probs(q2c, k2)
        return jnp.einsum('bhqk,bkhe->bqhe', p.astype(v.dtype), v)

    o = lax.map(one_chunk, (to_chunks(q1), to_chunks(q2), jnp.arange(nc)))
    return o.transpose(1, 0, 2, 3, 4).reshape(B, Tq, H, v.shape[-1])


def moba_attention(q, k, v, q_start):
    B, Tq, H, d = q.shape
    L = k.shape[1]
    nb = -(-L // MOBA_BLOCK)
    lp = nb * MOBA_BLOCK
    pad = ((0, 0), (0, lp - L), (0, 0), (0, 0))
    kb = jnp.pad(k, pad).reshape(B, nb, MOBA_BLOCK, H, d).transpose(0, 3, 1, 2, 4)
    vb = jnp.pad(v, pad).reshape(B, nb, MOBA_BLOCK, H, d).transpose(0, 3, 1, 2, 4)
    kmean = jnp.mean(kb.astype(jnp.float32), axis=3)
    qpos = q_start + jnp.arange(Tq)
    own = qpos // MOBA_BLOCK
    gate = jnp.einsum('bqhd,bhnd->bhqn', q.astype(jnp.float32), kmean)
    gate = jnp.where(jnp.arange(nb)[None, :] < own[:, None], gate, -jnp.inf)
    kk = min(MOBA_TOPK, max(nb - 1, 1))
    _, top = lax.top_k(gate, kk)
    sel = jnp.concatenate(
        [top.astype(jnp.int32), jnp.broadcast_to(own[None, None, :, None], (B, H, Tq, 1)).astype(jnp.int32)], axis=-1)
    rank_ok = jnp.arange(kk)[None, :] < own[:, None]
    own_ok = own[:, None] * MOBA_BLOCK + jnp.arange(MOBA_BLOCK)[None, :] <= qpos[:, None]
    mask = jnp.concatenate(
        [jnp.broadcast_to(rank_ok[:, :, None], (Tq, kk, MOBA_BLOCK)), own_ok[:, None, :]], axis=1)
    qcs = math.gcd(Tq, Q_CHUNK)
    nc = Tq // qcs
    maskc = mask.reshape(nc, qcs, kk + 1, MOBA_BLOCK)
    kflat = kb.reshape(B * H * nb, MOBA_BLOCK, d)
    vflat = vb.reshape(B * H * nb, MOBA_BLOCK, d)
    base = (jnp.arange(B * H, dtype=jnp.int32) * nb).reshape(B, H, 1, 1)
    fidx = (sel + base).reshape(B * H * nc, qcs, kk + 1)
    qf = q.transpose(0, 2, 1, 3).reshape(B * H * nc, qcs, d)
    cid = jnp.tile(jnp.arange(nc), B * H)
    scale = d ** -0.5

    def one(args):
        qi, fi, c = args
        kg = kflat[fi]
        vg = vflat[fi]
        s = jnp.einsum('qd,qjpd->qjp', qi, kg, preferred_element_type=jnp.float32) * scale
        s = jnp.where(maskc[c], s, -jnp.inf).reshape(qcs, (kk + 1) * MOBA_BLOCK)
        p = jax.nn.softmax(s, axis=-1).reshape(qcs, kk + 1, MOBA_BLOCK)
        return jnp.einsum('qjp,qjpd->qd', p.astype(vg.dtype), vg)

    o = lax.map(one, (qf, fidx, cid))
    return o.reshape(B, H, Tq, d).transpose(0, 2, 1, 3)


def mixer_layer(x, pos0, past_a, past_b, lam_init, norm_g, w_in, qn_a, kn_a, lq1, lk1, lq2, lk2,
                subln_g, qn_b, kn_b, b_merge, w_up_a, w_up_b, w_out):
    B, T, _ = x.shape
    pos = pos0 + jnp.arange(T)
    h = rms_norm(x, norm_g)
    z = jnp.einsum('btd,de->bte', h, w_in)
    sizes = [W_A, W_A, W_A, W_A, W_B, W_B, W_B, W_B, 2 * D_MODEL]
    points = [int(p) for p in np.cumsum(sizes)[:-1]]
    qa, ka, va, ga, qb, kb, vb, gb, gm = jnp.split(z, points, axis=-1)

    qa = rope(rms_norm(qa.reshape(B, T, 2 * H_A, D_A), qn_a), pos).reshape(B, T, H_A, 2, D_A)
    ka = rope(rms_norm(ka.reshape(B, T, 2 * H_A, D_A), kn_a), pos).reshape(B, T, H_A, 2 * D_A)
    new_a = jnp.stack([ka, va.reshape(B, T, H_A, 2 * D_A)], axis=2)
    full_a = new_a if past_a is None else jnp.concatenate([past_a, new_a], axis=1)
    La = full_a.shape[1]
    k_a = full_a[:, :, 0].reshape(B, La, H_A, 2, D_A)
    lam = (jnp.exp(jnp.sum(lq1.astype(jnp.float32) * lk1.astype(jnp.float32)))
           - jnp.exp(jnp.sum(lq2.astype(jnp.float32) * lk2.astype(jnp.float32))) + lam_init)
    oa = diff_attention(qa[..., 0, :], qa[..., 1, :], k_a[..., 0, :], k_a[..., 1, :], full_a[:, :, 1], lam, pos0)
    oa = rms_norm(oa, subln_g) * (1.0 - lam_init)
    ya = jnp.einsum('btw,wd->btd', oa.reshape(B, T, W_A) * jax.nn.silu(ga), w_up_a)

    qb = rope(rms_norm(qb.reshape(B, T, H_B, D_B), qn_b), pos)
    kb = rope(rms_norm(kb.reshape(B, T, H_B, D_B), kn_b), pos)
    new_b = jnp.stack([kb, vb.reshape(B, T, H_B, D_B)], axis=2)
    full_b = new_b if past_b is None else jnp.concatenate([past_b, new_b], axis=1)
    ob = moba_attention(qb, full_b[:, :, 0], full_b[:, :, 1], pos0)
    yb = jnp.einsum('btw,wd->btd', ob.reshape(B, T, W_B) * jax.nn.silu(gb), w_up_b)

    g_a, g_b = jnp.split(gm + b_merge, 2, axis=-1)
    merged = jax.nn.sigmoid(g_a) * ya + jax.nn.sigmoid(g_b) * yb
    return x + jnp.einsum('btd,de->bte', merged, w_out), new_a, new_b


def setup_inputs(seed: int = 0) -> dict:
    key = jax.random.key(seed)
    ks = jax.random.split(key, 24)
    f32 = jnp.float32
    n_pages = PAST_LEN // PAGE_SIZE
    used = DEC_BATCH * n_pages
    n_pool = used + max(1, used // 4)
    d_in = 4 * W_A + 4 * W_B + 2 * D_MODEL

    def gain(k, n):
        return 1.0 + 0.02 * jax.random.normal(k, (DEPTH, n), f32)

    page_table = jax.random.permutation(ks[3], n_pool)[:used].reshape(DEC_BATCH, n_pages).astype(jnp.int32)
    return {
        'x_prompt': jax.random.normal(ks[0], (BATCH, SEQ, D_MODEL), f32),
        'x_sample': jax.random.normal(ks[1], (DEC_BATCH, DEC_SEQ, D_MODEL), f32),
        'cache_kv_diff': jax.random.normal(ks[2], (DEPTH, n_pool, PAGE_SIZE, 2, H_A, 2 * D_A), f32),
        'cache_kv_moba': jax.random.normal(ks[4], (DEPTH, n_pool, PAGE_SIZE, 2, H_B, D_B), f32),
        'page_table': page_table,
        'norm_gain': gain(ks[5], D_MODEL),
        'w_in': jax.random.normal(ks[6], (DEPTH, D_MODEL, d_in), f32) * D_MODEL ** -0.5,
        'q_norm_a': gain(ks[7], D_A),
        'k_norm_a': gain(ks[8], D_A),
        'lambda_q1': 0.1 * jax.random.normal(ks[9], (DEPTH, D_A), f32),
        'lambda_k1': 0.1 * jax.random.normal(ks[10], (DEPTH, D_A), f32),
        'lambda_q2': 0.1 * jax.random.normal(ks[11], (DEPTH, D_A), f32),
        'lambda_k2': 0.1 * jax.random.normal(ks[12], (DEPTH, D_A), f32),
        'subln_gain': gain(ks[13], 2 * D_A),
        'q_norm_b': gain(ks[14], D_B),
        'k_norm_b': gain(ks[15], D_B),
        'b_merge': 0.02 * jax.random.normal(ks[16], (DEPTH, 2 * D_MODEL), f32),
        'w_up_a': jax.random.normal(ks[17], (DEPTH, W_A, D_MODEL), f32) * W_A ** -0.5,
        'w_up_b': jax.random.normal(ks[18], (DEPTH, W_B, D_MODEL), f32) * W_B ** -0.5,
        'w_out': jax.random.normal(ks[19], (DEPTH, D_MODEL, D_MODEL), f32) * D_MODEL ** -0.5,
    }


def reference(x_prompt, x_sample, cache_kv_diff, cache_kv_moba, page_table, norm_gain, w_in,
              q_norm_a, k_norm_a, lambda_q1, lambda_k1, lambda_q2, lambda_k2, subln_gain,
              q_norm_b, k_norm_b, b_merge, w_up_a, w_up_b, w_out):
    n_seq, n_pages = page_table.shape
    past_len = n_pages * cache_kv_diff.shape[2]
    yp, ys = x_prompt, x_sample
    kvd_p, kvm_p, kvd_s, kvm_s = [], [], [], []
    for l in range(DEPTH):
        lam_init = lambda_init_of(l)
        w = (norm_gain[l], w_in[l], q_norm_a[l], k_norm_a[l], lambda_q1[l], lambda_k1[l],
             lambda_q2[l], lambda_k2[l], subln_gain[l], q_norm_b[l], k_norm_b[l], b_merge[l],
             w_up_a[l], w_up_b[l], w_out[l])
        yp, na, nb_ = mixer_layer(yp, 0, None, None, lam_init, *w)
        past_a = cache_kv_diff[l, page_table].reshape(n_seq, past_len, 2, H_A, 2 * D_A)
        past_b = cache_kv_moba[l, page_table].reshape(n_seq, past_len, 2, H_B, D_B)
        ys, sa, sb = mixer_layer(ys, past_len, past_a, past_b, lam_init, *w)
        kvd_p.append(na)
        kvm_p.append(nb_)
        kvd_s.append(sa)
        kvm_s.append(sb)
    kv_diff_prompt = jnp.stack(kvd_p)
    kv_moba_prompt = jnp.stack(kvm_p)
    kv_diff_sample = jnp.stack(kvd_s)
    kv_moba_sample = jnp.stack(kvm_s)
    return (yp, ys, kv_diff_prompt, kv_moba_prompt, kv_diff_sample, kv_moba_sample)
```

```python
import functools
import math

import jax
import jax.numpy as jnp
from jax import lax
from jax.experimental import pallas as pl
from jax.experimental.pallas import tpu as pltpu

F32 = jnp.float32
BF16 = jnp.bfloat16

ROPE_THETA = 10000.0
NORM_EPS = 1e-6
N_HEADS = 8
HEAD_W = 128
BRANCH_W = N_HEADS * HEAD_W
D_A = 64
MOBA_BLOCK = 256
MOBA_TOPK = 3
LANES = 128
VMEM_LIMIT = 56 * 1024 * 1024

T_QA, T_KA, T_VA, T_GA, T_QB, T_KB, T_VB, T_GB, T_GM = 0, 1, 2, 3, 4, 5, 6, 7, 8
N_TILES = 12

_NT = (((1,), (1,)), ((), ()))


def _lambda_init(layer):
    return 0.8 - 0.6 * math.exp(-0.3 * layer)


def _norm_rope(x, gain, cos, sin, d):
    lane = lax.broadcasted_iota(jnp.int32, x.shape, 1)
    x2 = x * x
    if d == 64:
        lo = lane < 64
        s_lo = jnp.sum(jnp.where(lo, x2, 0.0), axis=-1, keepdims=True)
        s_hi = jnp.sum(jnp.where(lo, 0.0, x2), axis=-1, keepdims=True)
        ms = jnp.where(lo, s_lo, s_hi) * (1.0 / 64.0)
    else:
        ms = jnp.sum(x2, axis=-1, keepdims=True) * (1.0 / 128.0)
    y = x * lax.rsqrt(ms + NORM_EPS) * gain
    if d == 64:
        partner = jnp.where((lane & 63) < 32, pltpu.roll(y, 96, 1), pltpu.roll(y, 32, 1))
    else:
        partner = pltpu.roll(y, 64, 1)
    return y * cos + partner * sin


def _proj_kernel(x_ref, g_ref, w_ref, tab_ref, aux_ref, zb_ref, kvd_ref, kvm_ref, h_scr):
    n = pl.program_id(1)

    @pl.when(n == 0)
    def _():
        x = x_ref[...]
        ms = jnp.mean(x * x, axis=-1, keepdims=True)
        h_scr[...] = (x * lax.rsqrt(ms + NORM_EPS) * g_ref[...]).astype(BF16)

    z = jnp.dot(h_scr[...], w_ref[...], preferred_element_type=F32)
    aux = aux_ref[0]

    def put_norm_rope(d, q_scale, f32_ref):
        base = 0 if d == 64 else 2 * LANES
        cos = tab_ref[:, base:base + LANES]
        sin = tab_ref[:, base + LANES:base + 2 * LANES]
        for c in range(N_HEADS):
            sl = slice(c * HEAD_W, (c + 1) * HEAD_W)
            r = _norm_rope(z[:, sl], aux[:, sl], cos, sin, d)
            if f32_ref is not None:
                f32_ref[:, c, :] = r
            zb_ref[:, sl] = (r * q_scale).astype(zb_ref.dtype)

    def put_plain(f32_ref):
        for c in range(N_HEADS):
            f32_ref[:, c, :] = z[:, c * HEAD_W:(c + 1) * HEAD_W]
        zb_ref[...] = z.astype(zb_ref.dtype)

    def put_silu():
        zb_ref[...] = (z * jax.nn.sigmoid(z)).astype(zb_ref.dtype)

    def put_sigmoid():
        zb_ref[...] = jax.nn.sigmoid(z + aux).astype(zb_ref.dtype)

    pl.when(n == T_QA)(lambda: put_norm_rope(64, D_A ** -0.5, None))
    pl.when(n == T_KA)(lambda: put_norm_rope(64, 1.0, kvd_ref))
    pl.when(n == T_VA)(lambda: put_plain(kvd_ref))
    pl.when(n == T_QB)(lambda: put_norm_rope(128, 1.0, None))
    pl.when(n == T_KB)(lambda: put_norm_rope(128, 1.0, kvm_ref))
    pl.when(n == T_VB)(lambda: put_plain(kvm_ref))
    pl.when((n == T_GA) | (n == T_GB))(put_silu)
    pl.when(n >= T_GM)(put_sigmoid)


def _proj(x, g, w_bf, tab, aux, *, tm, z_dtype):
    rows, d_model = x.shape
    tab_blocks = tab.shape[0] // tm
    clip01 = lambda v: jnp.minimum(jnp.maximum(v, 0), 1)
    return pl.pallas_call(
        _proj_kernel,
        grid=(rows // tm, N_TILES),
        in_specs=[
            pl.BlockSpec((tm, d_model), lambda m, n: (m, 0)),
            pl.BlockSpec((1, d_model), lambda m, n: (0, 0)),
            pl.BlockSpec((d_model, BRANCH_W), lambda m, n: (0, n)),
            pl.BlockSpec((tm, 4 * LANES), lambda m, n: (m % tab_blocks, 0)),
            pl.BlockSpec((1, 1, BRANCH_W), lambda m, n: (n, 0, 0)),
        ],
        out_specs=[
            pl.BlockSpec((tm, BRANCH_W), lambda m, n: (m, n)),
            pl.BlockSpec((tm, N_HEADS, HEAD_W), lambda m, n: (m, clip01(n - T_KA), 0)),
            pl.BlockSpec((tm, N_HEADS, HEAD_W), lambda m, n: (m, clip01(n - T_KB), 0)),
        ],
        out_shape=[
            jax.ShapeDtypeStruct((rows, N_TILES * BRANCH_W), z_dtype),
            jax.ShapeDtypeStruct((rows, 2 * N_HEADS, HEAD_W), F32),
            jax.ShapeDtypeStruct((rows, 2 * N_HEADS, HEAD_W), F32),
        ],
        scratch_shapes=[pltpu.VMEM((tm, d_model), BF16)],
        compiler_params=pltpu.CompilerParams(
            dimension_semantics=("parallel", "arbitrary"), vmem_limit_bytes=VMEM_LIMIT),
        name="proj",
    )(x, g, w_bf, tab, aux)


def _lambda_full(lp_ref, lam_init):
    a = jnp.sum(lp_ref[0:1, :] * lp_ref[1:2, :], axis=-1, keepdims=True)
    b = jnp.sum(lp_ref[2:3, :] * lp_ref[3:4, :], axis=-1, keepdims=True)
    return jnp.exp(a) - jnp.exp(b) + lam_init


def _subln(o, gain, lam_init):
    ms = jnp.mean(o * o, axis=-1, keepdims=True)
    return o * lax.rsqrt(ms + NORM_EPS) * gain * (1.0 - lam_init)


def _softmax_first(s, v):
    m = jnp.max(s, axis=-1, keepdims=True)
    p = jnp.exp(s - m)
    l = jnp.sum(p, axis=-1, keepdims=True)
    acc = jnp.dot(p.astype(BF16), v, preferred_element_type=F32)
    return m, l, acc


def _softmax_step(s, v, m, l, acc):
    m_new = jnp.maximum(m, jnp.max(s, axis=-1, keepdims=True))
    a = jnp.exp(m - m_new)
    p = jnp.exp(s - m_new)
    l = a * l + jnp.sum(p, axis=-1, keepdims=True)
    acc = a * acc + jnp.dot(p.astype(BF16), v, preferred_element_type=F32)
    return m_new, l, acc


def _top_blocks(gate, n_valid, axis):
    idx = lax.broadcasted_iota(jnp.int32, gate.shape, axis)
    size = gate.shape[axis]
    g = jnp.where(idx < n_valid, gate, -jnp.inf)
    sel = jnp.zeros(gate.shape, F32)
    for r in range(MOBA_TOPK):
        mx = jnp.max(g, axis=axis, keepdims=True)
        pick = jnp.min(jnp.where(g == mx, idx, size), axis=axis, keepdims=True)
        hit = idx == pick
        sel = jnp.maximum(sel, jnp.where(hit, jnp.where(r < n_valid, 1.0, 0.0), 0.0))
        g = jnp.where(hit, -jnp.inf, g)
    return sel


def _diff_attn_kernel(q_ref, k_ref, v_ref, lp_ref, sg_ref, o_ref, *, tq, lam_init):
    qi = pl.program_id(2)
    q = q_ref[...]
    lane = lax.broadcasted_iota(jnp.int32, q.shape, 1)
    zero = jnp.zeros_like(q)
    q1 = jnp.where(lane < D_A, q, zero)
    q2 = jnp.where(lane < D_A, zero, q)

    def qk(qq, kc):
        return lax.dot_general(qq, kc, _NT, preferred_element_type=F32)

    start = pl.multiple_of(qi * tq, tq)
    kd = k_ref[pl.ds(start, tq), :]
    vd = v_ref[pl.ds(start, tq), :]
    row = lax.broadcasted_iota(jnp.int32, (tq, tq), 0)
    col = lax.broadcasted_iota(jnp.int32, (tq, tq), 1)
    causal = col <= row
    st1 = _softmax_first(jnp.where(causal, qk(q1, kd), -jnp.inf), vd)
    st2 = _softmax_first(jnp.where(causal, qk(q2, kd), -jnp.inf), vd)

    def body(j, carry):
        s1, s2 = carry
        off = pl.multiple_of(j * tq, tq)
        kc = k_ref[pl.ds(off, tq), :]
        vc = v_ref[pl.ds(off, tq), :]
        return (_softmax_step(qk(q1, kc), vc, *s1), _softmax_step(qk(q2, kc), vc, *s2))

    (_, l1, a1), (_, l2, a2) = lax.fori_loop(0, qi, body, (st1, st2))
    lam = _lambda_full(lp_ref, lam_init)
    o = a1 / l1 - lam * (a2 / l2)
    o_ref[...] = _subln(o, sg_ref[...], lam_init)


def _diff_attn(zb, lam_params, subln_g, *, batch, seq, tq, lam_init):
    nq = seq // tq
    kern = functools.partial(_diff_attn_kernel, tq=tq, lam_init=lam_init)
    return pl.pallas_call(
        kern,
        grid=(batch, N_HEADS, nq),
        in_specs=[
            pl.BlockSpec((tq, HEAD_W), lambda b, h, i: (b * nq + i, T_QA * N_HEADS + h)),
            pl.BlockSpec((seq, HEAD_W), lambda b, h, i: (b, T_KA * N_HEADS + h)),
            pl.BlockSpec((seq, HEAD_W), lambda b, h, i: (b, T_VA * N_HEADS + h)),
            pl.BlockSpec((4, D_A), lambda b, h, i: (0, 0)),
            pl.BlockSpec((1, HEAD_W), lambda b, h, i: (0, 0)),
        ],
        out_specs=pl.BlockSpec((tq, HEAD_W), lambda b, h, i: (b * nq + i, h)),
        out_shape=jax.ShapeDtypeStruct((batch * seq, BRANCH_W), F32),
        compiler_params=pltpu.CompilerParams(
            dimension_semantics=("parallel", "parallel", "arbitrary"),
            vmem_limit_bytes=VMEM_LIMIT),
        name="diff_attn",
    )(zb, zb, zb, lam_params, subln_g)


def _moba_attn_kernel(q_ref, k_ref, v_ref, o_ref, kmean_scr, *, nb):
    qi = pl.program_id(2)
    tq = MOBA_BLOCK
    scale = HEAD_W ** -0.5

    @pl.when(qi == 0)
    def _():
        kmean_scr[...] = jnp.zeros(kmean_scr.shape, F32)
        for j in range(nb):
            blk = k_ref[j * tq:(j + 1) * tq, :].astype(F32)
            kmean_scr[j:j + 1, :] = jnp.mean(blk, axis=0, keepdims=True)

    q = q_ref[...]
    gate = lax.dot_general(q.astype(F32), kmean_scr[...], _NT,
                           precision=lax.Precision.HIGHEST, preferred_element_type=F32)
    sel = _top_blocks(gate, qi, axis=1)
    blk_idx = lax.broadcasted_iota(jnp.int32, sel.shape, 1)

    def qk(kc):
        return lax.dot_general(q, kc, _NT, preferred_element_type=F32) * scale

    start = pl.multiple_of(qi * tq, tq)
    row = lax.broadcasted_iota(jnp.int32, (tq, tq), 0)
    col = lax.broadcasted_iota(jnp.int32, (tq, tq), 1)
    s_own = jnp.where(col <= row, qk(k_ref[pl.ds(start, tq), :]), -jnp.inf)
    st = _softmax_first(s_own, v_ref[pl.ds(start, tq), :])

    def body(j, carry):
        off = pl.multiple_of(j * tq, tq)
        picked = jnp.sum(jnp.where(blk_idx == j, sel, 0.0), axis=1, keepdims=True) > 0.5
        s = jnp.where(picked, qk(k_ref[pl.ds(off, tq), :]), -jnp.inf)
        return _softmax_step(s, v_ref[pl.ds(off, tq), :], *carry)

    _, l, acc = lax.fori_loop(0, qi, body, st)
    o_ref[...] = acc / l


def _moba_attn(zb, *, batch, seq):
    tq = MOBA_BLOCK
    nq = seq // tq
    assert nq <= LANES
    kern = functools.partial(_moba_attn_kernel, nb=nq)
    return pl.pallas_call(
        kern,
        grid=(batch, N_HEADS, nq),
        in_specs=[
            pl.BlockSpec((tq, HEAD_W), lambda b, h, i: (b * nq + i, T_QB * N_HEADS + h)),
            pl.BlockSpec((seq, HEAD_W), lambda b, h, i: (b, T_KB * N_HEADS + h)),
            pl.BlockSpec((seq, HEAD_W), lambda b, h, i: (b, T_VB * N_HEADS + h)),
        ],
        out_specs=pl.BlockSpec((tq, HEAD_W), lambda b, h, i: (b * nq + i, h)),
        out_shape=jax.ShapeDtypeStruct((batch * seq, BRANCH_W), F32),
        scratch_shapes=[pltpu.VMEM((LANES, HEAD_W), F32)],
        compiler_params=pltpu.CompilerParams(
            dimension_semantics=("parallel", "parallel", "arbitrary"),
            vmem_limit_bytes=VMEM_LIMIT),
        name="moba_attn",
    )(zb, zb, zb)


KV_ROWS = 2 * N_HEADS
Q_ROWS = 16


def _head_rows(ref, kv, h, page):
    return ref[pl.ds(kv * N_HEADS + h, page, stride=KV_ROWS), :].astype(BF16)


def _pad_rows(x_f32):
    return jnp.concatenate([x_f32, jnp.zeros((LANES - x_f32.shape[0], x_f32.shape[1]), F32)], axis=0)


def _head_queries(q_ref, h, halves):
    q = q_ref[:, HEAD_W * h:HEAD_W * (h + 1)]
    zero = jnp.zeros_like(q)
    if halves:
        lane = lax.broadcasted_iota(jnp.int32, q.shape, 1)
        q16 = jnp.concatenate([jnp.where(lane < D_A, q, zero), jnp.where(lane < D_A, zero, q)], axis=0)
    else:
        q16 = jnp.concatenate([q, zero], axis=0)
    return q16.astype(BF16)


def _own_scores(q16, kn_ref, h, dec):
    kn = _pad_rows(kn_ref[:, HEAD_W * h:HEAD_W * (h + 1)]).astype(BF16)
    s = lax.dot_general(q16, kn, _NT, preferred_element_type=F32)
    key = lax.broadcasted_iota(jnp.int32, s.shape, 1)
    t = lax.broadcasted_iota(jnp.int32, s.shape, 0) % dec
    return s, key <= t


def _diff_dec_kernel(pt_ref, q_ref, kn_ref, vn_ref, lp_ref, sg_ref, *rest, pages, page, lam_init):
    kv_refs = rest[:pages]
    o_ref = rest[pages]
    q_scr, m_scr, l_scr, acc_scr = rest[pages + 1:]
    c = pl.program_id(1)
    dec = q_ref.shape[0]

    @pl.when(c == 0)
    def _():
        for h in range(N_HEADS):
            q_scr[h] = _head_queries(q_ref, h, halves=True)
        m_scr[...] = jnp.full(m_scr.shape, -jnp.inf, F32)
        l_scr[...] = jnp.zeros(l_scr.shape, F32)
        acc_scr[...] = jnp.zeros(acc_scr.shape, F32)

    def update(h, s, v):
        m_old = m_scr[h]
        m_new = jnp.maximum(m_old, jnp.max(s, axis=1, keepdims=True))
        alpha = jnp.exp(m_old - m_new)
        p = jnp.exp(s - m_new[:, :1])
        l_scr[h] = alpha * l_scr[h] + jnp.sum(p, axis=1, keepdims=True)
        m_scr[h] = m_new
        acc_scr[h] = alpha * acc_scr[h] + jnp.dot(p.astype(BF16), v, preferred_element_type=F32)

    for h in range(N_HEADS):
        kc = jnp.concatenate([_head_rows(kv_refs[g], 0, h, page) for g in range(pages)], axis=0)
        vc = jnp.concatenate([_head_rows(kv_refs[g], 1, h, page) for g in range(pages)], axis=0)
        update(h, lax.dot_general(q_scr[h], kc, _NT, preferred_element_type=F32), vc)

    @pl.when(c == pl.num_programs(1) - 1)
    def _():
        lam = _lambda_full(lp_ref, lam_init)
        for h in range(N_HEADS):
            s, ok = _own_scores(q_scr[h], kn_ref, h, dec)
            vn = _pad_rows(vn_ref[:, HEAD_W * h:HEAD_W * (h + 1)]).astype(BF16)
            update(h, jnp.where(ok, s, -jnp.inf), vn)
            o = acc_scr[h] / l_scr[h]
            d = o[0:dec, :] - lam * o[8:8 + dec, :]
            o_ref[:, HEAD_W * h:HEAD_W * (h + 1)] = _subln(d, sg_ref[...], lam_init)


def _dec_specs(dec, page, pages, layer, tiles):
    row_specs = [pl.BlockSpec((dec, BRANCH_W), lambda b, c, pt, tile=tile: (b, tile))
                 for tile in tiles]
    cache_specs = [
        pl.BlockSpec((None, None, page * KV_ROWS, HEAD_W),
                     lambda b, c, pt, g=g: (layer, pt[b, c * pages + g], 0, 0))
        for g in range(pages)]
    return row_specs, cache_specs


def _diff_dec(page_table, zs, cache, lam_params, subln_g, *, layer, dec, pages, lam_init):
    n_seq, n_pages = page_table.shape
    page = cache.shape[2] // KV_ROWS
    assert dec == 8 and n_pages % pages == 0
    kern = functools.partial(_diff_dec_kernel, pages=pages, page=page, lam_init=lam_init)
    row_specs, cache_specs = _dec_specs(dec, page, pages, layer, (T_QA, T_KA, T_VA))
    state = pltpu.VMEM((N_HEADS, Q_ROWS, LANES), F32)
    return pl.pallas_call(
        kern,
        grid_spec=pltpu.PrefetchScalarGridSpec(
            num_scalar_prefetch=1,
            grid=(n_seq, n_pages // pages),
            in_specs=row_specs + [pl.BlockSpec((4, D_A), lambda b, c, pt: (0, 0)),
                                  pl.BlockSpec((1, HEAD_W), lambda b, c, pt: (0, 0))] + cache_specs,
            out_specs=pl.BlockSpec((dec, BRANCH_W), lambda b, c, pt: (b, 0)),
            scratch_shapes=[pltpu.VMEM((N_HEADS, Q_ROWS, HEAD_W), BF16), state, state, state]),
        out_shape=jax.ShapeDtypeStruct((n_seq * dec, BRANCH_W), F32),
        compiler_params=pltpu.CompilerParams(
            dimension_semantics=("parallel", "arbitrary"), vmem_limit_bytes=VMEM_LIMIT),
        name="diff_dec",
    )(page_table, zs, zs, zs, lam_params, subln_g, *([cache] * pages))


def _moba_dec_kernel(pt_ref, q_ref, kn_ref, vn_ref, *rest, pages, page, n_blocks):
    kv_refs = rest[:pages]
    o_ref = rest[pages]
    q_scr, gate_scr, m_scr, l_scr, acc_scr = rest[pages + 1:]
    c = pl.program_id(1)
    dec = q_ref.shape[0]
    scale = HEAD_W ** -0.5
    pages_per_block = MOBA_BLOCK // page
    blk_lane = lax.broadcasted_iota(jnp.int32, (Q_ROWS, LANES), 1)

    @pl.when(c == 0)
    def _():
        for h in range(N_HEADS):
            q_scr[h] = _head_queries(q_ref, h, halves=False)
        gate_scr[...] = jnp.zeros(gate_scr.shape, F32)
        m_scr[...] = jnp.zeros(m_scr.shape, F32)
        l_scr[...] = jnp.zeros(l_scr.shape, F32)

    def block_softmax(s, v):
        m = jnp.max(s, axis=1, keepdims=True)
        p = jnp.exp(s - m)
        return m, jnp.sum(p, axis=1, keepdims=True), jnp.dot(p.astype(BF16), v,
                                                             preferred_element_type=F32)

    for blk in range(pages // pages_per_block):
        j = c * (pages // pages_per_block) + blk
        here = blk_lane == j
        g_lo = blk * pages_per_block
        for h in range(N_HEADS):
            kc = jnp.concatenate([_head_rows(kv_refs[g_lo + g], 0, h, page)
                                  for g in range(pages_per_block)], axis=0)
            vc = jnp.concatenate([_head_rows(kv_refs[g_lo + g], 1, h, page)
                                  for g in range(pages_per_block)], axis=0)
            raw = lax.dot_general(q_scr[h], kc, _NT, preferred_element_type=F32)
            gate = jnp.mean(raw, axis=1, keepdims=True)
            m, l, acc = block_softmax(raw * scale, vc)
            gate_scr[h] = jnp.where(here, gate, gate_scr[h])
            m_scr[h] = jnp.where(here, m, m_scr[h])
            l_scr[h] = jnp.where(here, l, l_scr[h])
            acc_scr[h, j] = acc

    @pl.when(c == pl.num_programs(1) - 1)
    def _():
        for h in range(N_HEADS):
            s, ok = _own_scores(q_scr[h], kn_ref, h, dec)
            vn = _pad_rows(vn_ref[:, HEAD_W * h:HEAD_W * (h + 1)]).astype(BF16)
            m_own, l_own, acc_own = block_softmax(jnp.where(ok, s * scale, -jnp.inf), vn)
            picked = _top_blocks(gate_scr[h], n_blocks, axis=1) > 0.5
            m_all = m_scr[h]
            m_fin = jnp.maximum(m_own, jnp.max(jnp.where(picked, m_all, -jnp.inf),
                                               axis=1, keepdims=True))
            w = jnp.where(picked, jnp.exp(m_all - m_fin), 0.0)
            w_own = jnp.exp(m_own - m_fin)
            l_fin = w_own * l_own + jnp.sum(w * l_scr[h], axis=1, keepdims=True)
            acc = w_own * acc_own
            for jb in range(n_blocks):
                acc = acc + w[:, jb:jb + 1] * acc_scr[h, jb]
            o_ref[:, HEAD_W * h:HEAD_W * (h + 1)] = (acc / l_fin)[0:dec, :]


def _moba_dec(page_table, zs, cache, *, layer, dec, pages):
    n_seq, n_pages = page_table.shape
    page = cache.shape[2] // KV_ROWS
    past = n_pages * page
    assert dec == 8 and past % MOBA_BLOCK == 0 and MOBA_BLOCK % page == 0
    assert pages % (MOBA_BLOCK // page) == 0 and n_pages % pages == 0
    n_blocks = past // MOBA_BLOCK
    assert MOBA_TOPK <= n_blocks <= LANES
    kern = functools.partial(_moba_dec_kernel, pages=pages, page=page, n_blocks=n_blocks)
    row_specs, cache_specs = _dec_specs(dec, page, pages, layer, (T_QB, T_KB, T_VB))
    stat = pltpu.VMEM((N_HEADS, Q_ROWS, LANES), F32)
    return pl.pallas_call(
        kern,
        grid_spec=pltpu.PrefetchScalarGridSpec(
            num_scalar_prefetch=1,
            grid=(n_seq, n_pages // pages),
            in_specs=row_specs + cache_specs,
            out_specs=pl.BlockSpec((dec, BRANCH_W), lambda b, c, pt: (b, 0)),
            scratch_shapes=[pltpu.VMEM((N_HEADS, Q_ROWS, HEAD_W), BF16), stat, stat, stat,
                            pltpu.VMEM((N_HEADS, n_blocks, Q_ROWS, HEAD_W), F32)]),
        out_shape=jax.ShapeDtypeStruct((n_seq * dec, BRANCH_W), F32),
        compiler_params=pltpu.CompilerParams(
            dimension_semantics=("parallel", "arbitrary"), vmem_limit_bytes=VMEM_LIMIT),
        name="moba_dec",
    )(page_table, zs, zs, zs, *([cache] * pages))


def _out_kernel(x_ref, oa_ref, ob_ref, ga_ref, gb_ref, gm_ref, wa_ref, wb_ref, wo_ref, y_ref):
    d_model = x_ref.shape[1]
    ua = (oa_ref[...] * ga_ref[...].astype(F32)).astype(BF16)
    ub = (ob_ref[...] * gb_ref[...].astype(F32)).astype(BF16)
    ya = jnp.dot(ua, wa_ref[...], preferred_element_type=F32)
    yb = jnp.dot(ub, wb_ref[...], preferred_element_type=F32)
    merged = (gm_ref[:, :d_model].astype(F32) * ya + gm_ref[:, d_model:].astype(F32) * yb)
    y_ref[...] = x_ref[...] + jnp.dot(merged.astype(BF16), wo_ref[...],
                                      preferred_element_type=F32)


def _out(x, oa, ob, zb, wa, wb, wo, *, tm):
    rows, d_model = x.shape
    assert 2 * d_model == 4 * BRANCH_W
    const = lambda shape: pl.BlockSpec(shape, lambda m: (0, 0), pipeline_mode=pl.Buffered(1))
    return pl.pallas_call(
        _out_kernel,
        grid=(rows // tm,),
        in_specs=[
            pl.BlockSpec((tm, d_model), lambda m: (m, 0)),
            pl.BlockSpec((tm, BRANCH_W), lambda m: (m, 0)),
            pl.BlockSpec((tm, BRANCH_W), lambda m: (m, 0)),
            pl.BlockSpec((tm, BRANCH_W), lambda m: (m, T_GA)),
            pl.BlockSpec((tm, BRANCH_W), lambda m: (m, T_GB)),
            pl.BlockSpec((tm, 2 * d_model), lambda m: (m, T_GM * BRANCH_W // (2 * d_model))),
            const((BRANCH_W, d_model)),
            const((BRANCH_W, d_model)),
            const((d_model, d_model)),
        ],
        out_specs=pl.BlockSpec((tm, d_model), lambda m: (m, 0)),
        out_shape=jax.ShapeDtypeStruct((rows, d_model), F32),
        compiler_params=pltpu.CompilerParams(
            dimension_semantics=("parallel",), vmem_limit_bytes=VMEM_LIMIT),
        name="out",
    )(x, oa, ob, zb, zb, zb, wa, wb, wo)


def _rope_table(pos):
    pos = pos.astype(F32)[:, None]
    parts = []
    for d in (D_A, HEAD_W):
        half = d // 2
        inv = ROPE_THETA ** (-2.0 * jnp.arange(half, dtype=F32) / d)
        ang = pos * inv[None, :]
        cos, sin = jnp.cos(ang), jnp.sin(ang)
        reps = LANES // d
        parts.append(jnp.tile(jnp.concatenate([cos, cos], axis=-1), (1, reps)))
        parts.append(jnp.tile(jnp.concatenate([-sin, sin], axis=-1), (1, reps)))
    return jnp.concatenate(parts, axis=-1)


def _aux_rows(q_norm_a, k_norm_a, q_norm_b, k_norm_b, b_merge):
    zero = jnp.zeros((BRANCH_W,), F32)
    rows = [jnp.tile(q_norm_a, BRANCH_W // D_A), jnp.tile(k_norm_a, BRANCH_W // D_A), zero, zero,
            jnp.tile(q_norm_b, BRANCH_W // HEAD_W), jnp.tile(k_norm_b, BRANCH_W // HEAD_W),
            zero, zero]
    rows += list(b_merge.reshape(-1, BRANCH_W))
    return jnp.stack(rows)[:, None, :]


def kernel(x_prompt, x_sample, cache_kv_diff, cache_kv_moba, page_table, norm_gain, w_in, q_norm_a, k_norm_a, lambda_q1, lambda_k1, lambda_q2, lambda_k2, subln_gain, q_norm_b, k_norm_b, b_merge, w_up_a, w_up_b, w_out):
    batch, seq, d_model = x_prompt.shape
    n_seq, dec, _ = x_sample.shape
    depth, n_pool, page = cache_kv_diff.shape[:3]
    n_pages = page_table.shape[1]
    past = n_pages * page
    assert w_in.shape[2] == N_TILES * BRANCH_W and seq % MOBA_BLOCK == 0

    cache_d = cache_kv_diff.reshape(depth, n_pool, page * KV_ROWS, HEAD_W)
    cache_m = cache_kv_moba.reshape(depth, n_pool, page * KV_ROWS, HEAD_W)
    tab_p = _rope_table(jnp.arange(seq))
    tab_s = jnp.tile(_rope_table(past + jnp.arange(dec)), (n_seq, 1))

    yp = x_prompt.reshape(batch * seq, d_model)
    ys = x_sample.reshape(n_seq * dec, d_model)
    kvd_p, kvm_p, kvd_s, kvm_s = [], [], [], []
    for l in range(depth):
        lam_init = _lambda_init(l)
        g = norm_gain[l][None, :]
        w_in_l = w_in[l].astype(BF16)
        wa, wb, wo = w_up_a[l].astype(BF16), w_up_b[l].astype(BF16), w_out[l].astype(BF16)
        aux = _aux_rows(q_norm_a[l], k_norm_a[l], q_norm_b[l], k_norm_b[l], b_merge[l])
        lam_params = jnp.stack([lambda_q1[l], lambda_k1[l], lambda_q2[l], lambda_k2[l]])
        sg = subln_gain[l][None, :]

        zb, kvd, kvm = _proj(yp, g, w_in_l, tab_p, aux, tm=512, z_dtype=BF16)
        oa = _diff_attn(zb, lam_params, sg, batch=batch, seq=seq, tq=256, lam_init=lam_init)
        ob = _moba_attn(zb, batch=batch, seq=seq)
        yp = _out(yp, oa, ob, zb, wa, wb, wo, tm=256)
        kvd_p.append(kvd)
        kvm_p.append(kvm)

        zs, kvd, kvm = _proj(ys, g, w_in_l, tab_s, aux, tm=n_seq * dec, z_dtype=F32)
        oa = _diff_dec(page_table, zs, cache_d, lam_params, sg, layer=l, dec=dec, pages=8,
                       lam_init=lam_init)
        ob = _moba_dec(page_table, zs, cache_m, layer=l, dec=dec, pages=4)
        ys = _out(ys, oa, ob, zs, wa, wb, wo, tm=n_seq * dec)
        kvd_s.append(kvd)
        kvm_s.append(kvm)

    kv_shape = lambda rows_a, rows_b: (depth, rows_a, rows_b, 2, N_HEADS, HEAD_W)
    return (yp.reshape(batch, seq, d_model),
            ys.reshape(n_seq, dec, d_model),
            jnp.stack(kvd_p).reshape(kv_shape(batch, seq)),
            jnp.stack(kvm_p).reshape(kv_shape(batch, seq)),
            jnp.stack(kvd_s).reshape(kv_shape(n_seq, dec)),
            jnp.stack(kvm_s).reshape(kv_shape(n_seq, dec)))
```

```python
import functools
import math

import jax
import jax.numpy as jnp
from jax import lax
from jax.experimental import pallas as pl
from jax.experimental.pallas import tpu as pltpu

F32 = jnp.float32
BF16 = jnp.bfloat16

ROPE_THETA = 10000.0
NORM_EPS = 1e-6
N_HEADS = 8
HEAD_W = 128
BRANCH_W = N_HEADS * HEAD_W
D_A = 64
MOBA_BLOCK = 256
MOBA_TOPK = 3
LANES = 128
ATT_TILE = 256
ATT_HEADS = 4
KV_ROWS = 2 * N_HEADS
VMEM_LIMIT = 56 * 1024 * 1024

T_QA, T_KA, T_VA, T_GA, T_QB, T_KB, T_VB, T_GB, T_GM = 0, 1, 2, 3, 4, 5, 6, 7, 8
N_TILES = 12

_NT = (((1,), (1,)), ((), ()))


def _lambda_init(layer):
    return 0.8 - 0.6 * math.exp(-0.3 * layer)


def _norm_rope(x, gain, cos, sin, d):
    lane = lax.broadcasted_iota(jnp.int32, x.shape, 1)
    x2 = x * x
    if d == 64:
        lo = lane < 64
        s_lo = jnp.sum(jnp.where(lo, x2, 0.0), axis=-1, keepdims=True)
        s_hi = jnp.sum(jnp.where(lo, 0.0, x2), axis=-1, keepdims=True)
        ms = jnp.where(lo, s_lo, s_hi) * (1.0 / 64.0)
    else:
        ms = jnp.sum(x2, axis=-1, keepdims=True) * (1.0 / 128.0)
    y = x * lax.rsqrt(ms + NORM_EPS) * gain
    if d == 64:
        partner = jnp.where((lane & 63) < 32, pltpu.roll(y, 96, 1), pltpu.roll(y, 32, 1))
    else:
        partner = pltpu.roll(y, 64, 1)
    return y * cos + partner * sin


def _proj_kernel(*refs, emit_vt, aliased):
    x_ref, g_ref, w_ref, tab_ref, aux_ref = refs[:5]
    outs = refs[5 + (2 if aliased else 0):]
    zb_ref, kvd_ref, kvm_ref = outs[:3]
    vt_ref = outs[3] if emit_vt else None
    h_scr = outs[-1]
    n = pl.program_id(1)

    @pl.when(n == 0)
    def _():
        x = x_ref[...]
        ms = jnp.mean(x * x, axis=-1, keepdims=True)
        h_scr[...] = (x * lax.rsqrt(ms + NORM_EPS) * g_ref[...]).astype(BF16)

    z = jnp.dot(h_scr[...], w_ref[...], preferred_element_type=F32)
    aux = aux_ref[0]

    def put_norm_rope(d, q_scale, f32_ref):
        base = 0 if d == 64 else 2 * LANES
        cos = tab_ref[:, base:base + LANES]
        sin = tab_ref[:, base + LANES:base + 2 * LANES]
        for c in range(N_HEADS):
            sl = slice(c * HEAD_W, (c + 1) * HEAD_W)
            r = _norm_rope(z[:, sl], aux[:, sl], cos, sin, d)
            if f32_ref is not None:
                f32_ref[:, c, :] = r
            zb_ref[:, sl] = (r * q_scale).astype(zb_ref.dtype)

    def put_values(f32_ref):
        for c in range(N_HEADS):
            zc = z[:, c * HEAD_W:(c + 1) * HEAD_W]
            f32_ref[:, c, :] = zc
            if emit_vt:
                for u in range(z.shape[0] // ATT_TILE):
                    vt_ref[c, u] = zc[u * ATT_TILE:(u + 1) * ATT_TILE, :].T.astype(BF16)
        zb_ref[...] = z.astype(zb_ref.dtype)

    def put_silu():
        zb_ref[...] = (z * jax.nn.sigmoid(z)).astype(zb_ref.dtype)

    def put_sigmoid():
        zb_ref[...] = jax.nn.sigmoid(z + aux).astype(zb_ref.dtype)

    pl.when(n == T_QA)(lambda: put_norm_rope(64, D_A ** -0.5, None))
    pl.when(n == T_KA)(lambda: put_norm_rope(64, 1.0, kvd_ref))
    pl.when(n == T_VA)(lambda: put_values(kvd_ref))
    pl.when(n == T_QB)(lambda: put_norm_rope(128, 1.0, None))
    pl.when(n == T_KB)(lambda: put_norm_rope(128, 1.0, kvm_ref))
    pl.when(n == T_VB)(lambda: put_values(kvm_ref))
    pl.when((n == T_GA) | (n == T_GB))(put_silu)
    pl.when(n >= T_GM)(put_sigmoid)


def _proj(x, g, w_bf, tab, aux, kv_bufs, *, layer, depth, tm, z_dtype, vt_seq=None):
    rows, d_model = x.shape
    tab_blocks = tab.shape[0] // tm
    emit_vt = vt_seq is not None
    aliased = kv_bufs is not None
    clip01 = lambda v: jnp.minimum(jnp.maximum(v, 0), 1)
    in_specs = [
        pl.BlockSpec((tm, d_model), lambda m, n: (m, 0)),
        pl.BlockSpec((1, d_model), lambda m, n: (0, 0)),
        pl.BlockSpec((d_model, BRANCH_W), lambda m, n: (0, n)),
        pl.BlockSpec((tm, 4 * LANES), lambda m, n: (m % tab_blocks, 0)),
        pl.BlockSpec((1, 1, BRANCH_W), lambda m, n: (n, 0, 0)),
    ]
    args = [x, g, w_bf, tab, aux]
    if aliased:
        in_specs += [pl.BlockSpec(memory_space=pl.ANY)] * 2
        args += list(kv_bufs)
    kv_sds = jax.ShapeDtypeStruct((depth, rows, KV_ROWS, HEAD_W), F32)
    out_specs = [
        pl.BlockSpec((tm, BRANCH_W), lambda m, n: (m, n)),
        pl.BlockSpec((None, tm, N_HEADS, HEAD_W), lambda m, n: (layer, m, clip01(n - T_KA), 0)),
        pl.BlockSpec((None, tm, N_HEADS, HEAD_W), lambda m, n: (layer, m, clip01(n - T_KB), 0)),
    ]
    out_shape = [jax.ShapeDtypeStruct((rows, N_TILES * BRANCH_W), z_dtype), kv_sds, kv_sds]
    if emit_vt:
        assert vt_seq % tm == 0 and tm % ATT_TILE == 0
        mpb = vt_seq // tm
        out_specs.append(pl.BlockSpec(
            (None, N_HEADS, tm // ATT_TILE, HEAD_W, ATT_TILE),
            lambda m, n: (m // mpb, jnp.where(n >= T_VB, 1, 0), m % mpb, 0, 0)))
        out_shape.append(jax.ShapeDtypeStruct(
            (rows // vt_seq, 2 * N_HEADS, vt_seq // ATT_TILE, HEAD_W, ATT_TILE), BF16))
    return pl.pallas_call(
        functools.partial(_proj_kernel, emit_vt=emit_vt, aliased=aliased),
        grid=(rows // tm, N_TILES),
        in_specs=in_specs,
        out_specs=out_specs,
        out_shape=out_shape,
        input_output_aliases={5: 1, 6: 2} if aliased else {},
        scratch_shapes=[pltpu.VMEM((tm, d_model), BF16)],
        compiler_params=pltpu.CompilerParams(
            dimension_semantics=("parallel", "arbitrary"), vmem_limit_bytes=VMEM_LIMIT),
        name="proj",
    )(*args)


def _lambda_full(lp_ref, lam_init):
    a = jnp.sum(lp_ref[0:1, :] * lp_ref[1:2, :], axis=-1, keepdims=True)
    b = jnp.sum(lp_ref[2:3, :] * lp_ref[3:4, :], axis=-1, keepdims=True)
    return jnp.exp(a) - jnp.exp(b) + lam_init


def _subln(o, gain, lam_init):
    ms = jnp.mean(o * o, axis=-1, keepdims=True)
    return o * lax.rsqrt(ms + NORM_EPS) * gain * (1.0 - lam_init)


def _softmax_first(s, vt, acc_scr):
    m = jnp.max(s, axis=0, keepdims=True)
    p = jnp.exp(s - m)
    l = jnp.sum(p, axis=0, keepdims=True)
    acc_scr[...] = jnp.dot(vt, p.astype(BF16), preferred_element_type=F32)
    return m, l


def _softmax_step(s, vt, m, l, acc_scr):
    m_new = jnp.maximum(m, jnp.max(s, axis=0, keepdims=True))
    a = jnp.exp(m - m_new)
    p = jnp.exp(s - m_new)
    l = a * l + jnp.sum(p, axis=0, keepdims=True)
    acc_scr[...] = a * acc_scr[...] + jnp.dot(vt, p.astype(BF16), preferred_element_type=F32)
    return m_new, l


def _top_blocks(gate, n_valid, axis):
    idx = lax.broadcasted_iota(jnp.int32, gate.shape, axis)
    size = gate.shape[axis]
    g = jnp.where(idx < n_valid, gate, -jnp.inf)
    sel = jnp.zeros(gate.shape, F32)
    for r in range(MOBA_TOPK):
        mx = jnp.max(g, axis=axis, keepdims=True)
        pick = jnp.min(jnp.where(g == mx, idx, size), axis=axis, keepdims=True)
        hit = idx == pick
        sel = jnp.maximum(sel, jnp.where(hit, jnp.where(r < n_valid, 1.0, 0.0), 0.0))
        g = jnp.where(hit, -jnp.inf, g)
    return sel


def _head_cols(g):
    return slice(g * HEAD_W, (g + 1) * HEAD_W)


def _diff_attn_kernel(q_ref, k_ref, vt_ref, lp_ref, sg_ref, o_ref, acc_scr, *, lam_init):
    qi = pl.program_id(2)
    tq = ATT_TILE
    key = lax.broadcasted_iota(jnp.int32, (tq, 2 * tq), 0)
    qcol = lax.broadcasted_iota(jnp.int32, (tq, 2 * tq), 1) % tq
    lane = lax.broadcasted_iota(jnp.int32, (tq, HEAD_W), 1)

    q12, carries = [], []
    for g in range(ATT_HEADS):
        q = q_ref[:, _head_cols(g)]
        zero = jnp.zeros_like(q)
        q12.append(jnp.concatenate([jnp.where(lane < D_A, q, zero),
                                    jnp.where(lane < D_A, zero, q)], axis=0))

    def scores(g, j):
        off = pl.multiple_of(j * tq, tq)
        return lax.dot_general(k_ref[pl.ds(off, tq), _head_cols(g)], q12[g], _NT,
                               preferred_element_type=F32)

    for g in range(ATT_HEADS):
        carries.append(_softmax_first(jnp.where(key <= qcol, scores(g, qi), -jnp.inf),
                                      vt_ref[g, qi], acc_scr.at[g]))

    def body(j, carries):
        ss = [scores(g, j) for g in range(ATT_HEADS)]
        return tuple(_softmax_step(ss[g], vt_ref[g, j], *carries[g], acc_scr.at[g])
                     for g in range(ATT_HEADS))

    carries = lax.fori_loop(0, qi, body, tuple(carries))
    lam = _lambda_full(lp_ref, lam_init)
    for g in range(ATT_HEADS):
        o = acc_scr[g] / carries[g][1]
        d = (o[:, :tq] - lam * o[:, tq:]).T
        o_ref[:, _head_cols(g)] = _subln(d, sg_ref[...], lam_init)


def _attn_specs(branch, seq, nq):
    q_tile, k_tile = (T_QA, T_KA) if branch == 0 else (T_QB, T_KB)
    groups = N_HEADS // ATT_HEADS
    width = ATT_HEADS * HEAD_W
    return [
        pl.BlockSpec((ATT_TILE, width), lambda b, h, i: (b * nq + i, q_tile * groups + h)),
        pl.BlockSpec((seq, width), lambda b, h, i: (b, k_tile * groups + h)),
        pl.BlockSpec((None, ATT_HEADS, nq, HEAD_W, ATT_TILE),
                     lambda b, h, i: (b, branch * groups + h, 0, 0, 0)),
    ]


def _attn_call(kern, name, branch, extra_specs, scratch, args, *, batch, seq):
    nq = seq // ATT_TILE
    width = ATT_HEADS * HEAD_W
    return pl.pallas_call(
        kern,
        grid=(batch, N_HEADS // ATT_HEADS, nq),
        in_specs=_attn_specs(branch, seq, nq) + extra_specs,
        out_specs=pl.BlockSpec((ATT_TILE, width), lambda b, h, i: (b * nq + i, h)),
        out_shape=jax.ShapeDtypeStruct((batch * seq, BRANCH_W), F32),
        scratch_shapes=scratch,
        compiler_params=pltpu.CompilerParams(
            dimension_semantics=("parallel", "parallel", "arbitrary"),
            vmem_limit_bytes=VMEM_LIMIT),
        name=name,
    )(*args)


def _diff_attn(zb, vt, lam_params, subln_g, *, batch, seq, lam_init):
    kern = functools.partial(_diff_attn_kernel, lam_init=lam_init)
    extra = [pl.BlockSpec((4, D_A), lambda b, h, i: (0, 0)),
             pl.BlockSpec((1, HEAD_W), lambda b, h, i: (0, 0))]
    scratch = [pltpu.VMEM((ATT_HEADS, HEAD_W, 2 * ATT_TILE), F32)]
    return _attn_call(kern, "diff_attn", 0, extra, scratch, (zb, zb, vt, lam_params, subln_g),
                      batch=batch, seq=seq)


def _moba_attn_kernel(q_ref, k_ref, vt_ref, o_ref, kmean_scr, sel_scr, acc_scr, *, nb):
    qi = pl.program_id(2)
    tq = ATT_TILE
    scale = HEAD_W ** -0.5

    @pl.when(qi == 0)
    def _():
        kmean_scr[...] = jnp.zeros(kmean_scr.shape, F32)
        for g in range(ATT_HEADS):
            for j in range(nb):
                blk = k_ref[j * tq:(j + 1) * tq, _head_cols(g)].astype(F32)
                kmean_scr[g, j:j + 1, :] = jnp.mean(blk, axis=0, keepdims=True)

    key = lax.broadcasted_iota(jnp.int32, (tq, tq), 0)
    qcol = lax.broadcasted_iota(jnp.int32, (tq, tq), 1)
    qs = [q_ref[:, _head_cols(g)] for g in range(ATT_HEADS)]

    def scores(g, j):
        off = pl.multiple_of(j * tq, tq)
        return lax.dot_general(k_ref[pl.ds(off, tq), _head_cols(g)], qs[g], _NT,
                               preferred_element_type=F32) * scale

    carries = []
    for g in range(ATT_HEADS):
        gate = lax.dot_general(kmean_scr[g], qs[g].astype(F32), _NT,
                               precision=lax.Precision.HIGHEST, preferred_element_type=F32)
        sel_scr[g] = _top_blocks(gate, qi, axis=0)
        carries.append(_softmax_first(jnp.where(key <= qcol, scores(g, qi), -jnp.inf),
                                      vt_ref[g, qi], acc_scr.at[g]))

    def body(j, carries):
        ss = [scores(g, j) for g in range(ATT_HEADS)]
        out = []
        for g in range(ATT_HEADS):
            picked = sel_scr[g, pl.ds(j, 1), :] > 0.5
            out.append(_softmax_step(jnp.where(picked, ss[g], -jnp.inf), vt_ref[g, j],
                                     *carries[g], acc_scr.at[g]))
        return tuple(out)

    carries = lax.fori_loop(0, qi, body, tuple(carries))
    for g in range(ATT_HEADS):
        o_ref[:, _head_cols(g)] = (acc_scr[g] / carries[g][1]).T


def _moba_attn(zb, vt, *, batch, seq):
    assert ATT_TILE == MOBA_BLOCK
    nq = seq // ATT_TILE
    nb_pad = -(-nq // 8) * 8
    kern = functools.partial(_moba_attn_kernel, nb=nq)
    scratch = [pltpu.VMEM((ATT_HEADS, nb_pad, HEAD_W), F32),
               pltpu.VMEM((ATT_HEADS, nb_pad, ATT_TILE), F32),
               pltpu.VMEM((ATT_HEADS, HEAD_W, ATT_TILE), F32)]
    return _attn_call(kern, "moba_attn", 1, [], scratch, (zb, zb, vt), batch=batch, seq=seq)


Q_ROWS = 16


def _state_rows(h):
    return slice(Q_ROWS * h, Q_ROWS * (h + 1))


def _cached_rows(kv_refs, gs, kv):
    return jnp.concatenate(
        [kv_refs[g][:, kv].reshape(-1, HEAD_W).astype(BF16) for g in gs], axis=0)


def _own_head_mask():
    row_head = lax.broadcasted_iota(jnp.int32, (N_HEADS * Q_ROWS, LANES), 0) // Q_ROWS
    col_head = lax.broadcasted_iota(jnp.int32, (N_HEADS * Q_ROWS, LANES), 1) % N_HEADS
    return row_head == col_head


def _mask_cols(y, mask, fill):
    return jnp.concatenate(
        [jnp.where(mask, y[:, t * LANES:(t + 1) * LANES], fill) for t in range(y.shape[1] // LANES)],
        axis=1)


def _own_p_times_v(pb, vn_ref):
    return jnp.concatenate(
        [jnp.dot(pb[_state_rows(h), :], _pad_rows(vn_ref[:, _head_cols(h)]).astype(BF16),
                 preferred_element_type=F32) for h in range(N_HEADS)], axis=0)


def _pad_rows(x_f32):
    return jnp.concatenate([x_f32, jnp.zeros((LANES - x_f32.shape[0], x_f32.shape[1]), F32)], axis=0)


def _head_queries(q_ref, h, halves):
    q = q_ref[:, HEAD_W * h:HEAD_W * (h + 1)]
    zero = jnp.zeros_like(q)
    if halves:
        lane = lax.broadcasted_iota(jnp.int32, q.shape, 1)
        q16 = jnp.concatenate([jnp.where(lane < D_A, q, zero), jnp.where(lane < D_A, zero, q)], axis=0)
    else:
        q16 = jnp.concatenate([q, zero], axis=0)
    return q16.astype(BF16)


def _own_scores(q16, kn_ref, h, dec):
    kn = _pad_rows(kn_ref[:, HEAD_W * h:HEAD_W * (h + 1)]).astype(BF16)
    s = lax.dot_general(q16, kn, _NT, preferred_element_type=F32)
    key = lax.broadcasted_iota(jnp.int32, s.shape, 1)
    t = lax.broadcasted_iota(jnp.int32, s.shape, 0) % dec
    return s, key <= t


def _diff_dec_kernel(pt_ref, q_ref, kn_ref, vn_ref, lp_ref, sg_ref, *rest, pages, lam_init):
    kv_refs = rest[:pages]
    o_ref = rest[pages]
    q_scr, m_scr, l_scr, acc_scr = rest[pages + 1:]
    c = pl.program_id(1)
    dec = q_ref.shape[0]

    @pl.when(c == 0)
    def _():
        for h in range(N_HEADS):
            q_scr[_state_rows(h), :] = _head_queries(q_ref, h, halves=True)
        m_scr[...] = jnp.full(m_scr.shape, -jnp.inf, F32)
        l_scr[...] = jnp.zeros(l_scr.shape, F32)
        acc_scr[...] = jnp.zeros(acc_scr.shape, F32)

    def update(s, p_times_v):
        m_old = m_scr[...]
        m_new = jnp.maximum(m_old, jnp.max(s, axis=1, keepdims=True))
        alpha = jnp.exp(m_old - m_new)
        p = jnp.exp(s - m_new[:, :1])
        l_scr[...] = alpha * l_scr[...] + jnp.sum(p, axis=1, keepdims=True)
        m_scr[...] = m_new
        acc_scr[...] = alpha * acc_scr[...] + p_times_v(p.astype(BF16))

    halves = [range(0, pages // 2), range(pages // 2, pages)]
    own_head = _own_head_mask()
    ys = [lax.dot_general(q_scr[...], _cached_rows(kv_refs, gs, 0), _NT,
                          preferred_element_type=F32) for gs in halves]
    for gs, y in zip(halves, ys):
        v_all = _cached_rows(kv_refs, gs, 1)
        update(_mask_cols(y, own_head, -jnp.inf),
               lambda pb: jnp.dot(pb, v_all, preferred_element_type=F32))

    @pl.when(c == pl.num_programs(1) - 1)
    def _():
        owns = [_own_scores(q_scr[_state_rows(h), :], kn_ref, h, dec) for h in range(N_HEADS)]
        s = jnp.concatenate([jnp.where(ok, s_h, -jnp.inf) for s_h, ok in owns], axis=0)
        update(s, lambda pb: _own_p_times_v(pb, vn_ref))
        o = acc_scr[...] / l_scr[...]
        lam = _lambda_full(lp_ref, lam_init)
        for h in range(N_HEADS):
            r = Q_ROWS * h
            d = o[r:r + dec, :] - lam * o[r + 8:r + 8 + dec, :]
            o_ref[:, _head_cols(h)] = _subln(d, sg_ref[...], lam_init)


def _dec_specs(dec, page, pages, layer, tiles):
    row_specs = [pl.BlockSpec((dec, BRANCH_W), lambda b, c, pt, tile=tile: (b, tile))
                 for tile in tiles]
    cache_specs = [
        pl.BlockSpec((None, None, page, 2, N_HEADS, HEAD_W),
                     lambda b, c, pt, g=g: (layer, pt[b, c * pages + g], 0, 0, 0, 0))
        for g in range(pages)]
    return row_specs, cache_specs


def _diff_dec(page_table, zs, cache, lam_params, subln_g, *, layer, dec, pages, lam_init):
    n_seq, n_pages = page_table.shape
    page = cache.shape[2]
    assert dec == 8 and n_pages % pages == 0 and pages % 2 == 0
    kern = functools.partial(_diff_dec_kernel, pages=pages, lam_init=lam_init)
    row_specs, cache_specs = _dec_specs(dec, page, pages, layer, (T_QA, T_KA, T_VA))
    state = pltpu.VMEM((N_HEADS * Q_ROWS, LANES), F32)
    return pl.pallas_call(
        kern,
        grid_spec=pltpu.PrefetchScalarGridSpec(
            num_scalar_prefetch=1,
            grid=(n_seq, n_pages // pages),
            in_specs=row_specs + [pl.BlockSpec((4, D_A), lambda b, c, pt: (0, 0)),
                                  pl.BlockSpec((1, HEAD_W), lambda b, c, pt: (0, 0))] + cache_specs,
            out_specs=pl.BlockSpec((dec, BRANCH_W), lambda b, c, pt: (b, 0)),
            scratch_shapes=[pltpu.VMEM((N_HEADS * Q_ROWS, HEAD_W), BF16), state, state, state]),
        out_shape=jax.ShapeDtypeStruct((n_seq * dec, BRANCH_W), F32),
        compiler_params=pltpu.CompilerParams(
            dimension_semantics=("parallel", "arbitrary"), vmem_limit_bytes=VMEM_LIMIT),
        name="diff_dec",
    )(page_table, zs, zs, zs, lam_params, subln_g, *([cache] * pages))


def _moba_dec_kernel(pt_ref, q_ref, kn_ref, vn_ref, *rest, pages, page, n_blocks):
    kv_refs = rest[:pages]
    o_ref = rest[pages]
    q_scr, gate_scr, m_scr, l_scr, acc_scr = rest[pages + 1:]
    c = pl.program_id(1)
    dec = q_ref.shape[0]
    scale = HEAD_W ** -0.5
    pages_per_block = MOBA_BLOCK // page
    blocks_per_step = pages // pages_per_block
    blk_lane = lax.broadcasted_iota(jnp.int32, (N_HEADS * Q_ROWS, LANES), 1)

    @pl.when(c == 0)
    def _():
        for h in range(N_HEADS):
            q_scr[_state_rows(h), :] = _head_queries(q_ref, h, halves=False)
        gate_scr[...] = jnp.zeros(gate_scr.shape, F32)
        m_scr[...] = jnp.zeros(m_scr.shape, F32)
        l_scr[...] = jnp.zeros(l_scr.shape, F32)

    def block_softmax(s, p_times_v):
        m = jnp.max(s, axis=1, keepdims=True)
        p = jnp.exp(s - m)
        return m, jnp.sum(p, axis=1, keepdims=True), p_times_v(p.astype(BF16))

    def block_pages(blk):
        return range(blk * pages_per_block, (blk + 1) * pages_per_block)

    own_head = _own_head_mask()
    ys = [lax.dot_general(q_scr[...], _cached_rows(kv_refs, block_pages(blk), 0), _NT,
                          preferred_element_type=F32) for blk in range(blocks_per_step)]
    for blk in range(blocks_per_step):
        j = c * blocks_per_step + blk
        here = blk_lane == j
        gate = jnp.sum(_mask_cols(ys[blk], own_head, 0.0), axis=1, keepdims=True) * (1.0 / MOBA_BLOCK)
        v_all = _cached_rows(kv_refs, block_pages(blk), 1)
        m, l, acc = block_softmax(_mask_cols(ys[blk] * scale, own_head, -jnp.inf),
                                  lambda pb: jnp.dot(pb, v_all, preferred_element_type=F32))
        gate_scr[...] = jnp.where(here, gate, gate_scr[...])
        m_scr[...] = jnp.where(here, m, m_scr[...])
        l_scr[...] = jnp.where(here, l, l_scr[...])
        acc_scr[j] = acc

    @pl.when(c == pl.num_programs(1) - 1)
    def _():
        owns = [_own_scores(q_scr[_state_rows(h), :], kn_ref, h, dec) for h in range(N_HEADS)]
        s = jnp.concatenate([jnp.where(ok, s_h * scale, -jnp.inf) for s_h, ok in owns], axis=0)
        m_own, l_own, acc_own = block_softmax(s, lambda pb: _own_p_times_v(pb, vn_ref))
        picked = _top_blocks(gate_scr[...], n_blocks, axis=1) > 0.5
        m_all = m_scr[...]
        m_fin = jnp.maximum(m_own, jnp.max(jnp.where(picked, m_all, -jnp.inf),
                                           axis=1, keepdims=True))
        w = jnp.where(picked, jnp.exp(m_all - m_fin), 0.0)
        w_own = jnp.exp(m_own - m_fin)
        l_fin = w_own * l_own + jnp.sum(w * l_scr[...], axis=1, keepdims=True)
        acc = w_own * acc_own
        for jb in range(n_blocks):
            acc = acc + w[:, jb:jb + 1] * acc_scr[jb]
        o = acc / l_fin
        for h in range(N_HEADS):
            o_ref[:, _head_cols(h)] = o[Q_ROWS * h:Q_ROWS * h + dec, :]


def _moba_dec(page_table, zs, cache, *, layer, dec, pages):
    n_seq, n_pages = page_table.shape
    page = cache.shape[2]
    past = n_pages * page
    assert dec == 8 and past % MOBA_BLOCK == 0 and MOBA_BLOCK % page == 0
    assert pages % (MOBA_BLOCK // page) == 0 and n_pages % pages == 0
    n_blocks = past // MOBA_BLOCK
    assert MOBA_TOPK <= n_blocks <= LANES
    kern = functools.partial(_moba_dec_kernel, pages=pages, page=page, n_blocks=n_blocks)
    row_specs, cache_specs = _dec_specs(dec, page, pages, layer, (T_QB, T_KB, T_VB))
    stat = pltpu.VMEM((N_HEADS * Q_ROWS, LANES), F32)
    return pl.pallas_call(
        kern,
        grid_spec=pltpu.PrefetchScalarGridSpec(
            num_scalar_prefetch=1,
            grid=(n_seq, n_pages // pages),
            in_specs=row_specs + cache_specs,
            out_specs=pl.BlockSpec((dec, BRANCH_W), lambda b, c, pt: (b, 0)),
            scratch_shapes=[pltpu.VMEM((N_HEADS * Q_ROWS, HEAD_W), BF16), stat, stat, stat,
                            pltpu.VMEM((n_blocks, N_HEADS * Q_ROWS, HEAD_W), F32)]),
        out_shape=jax.ShapeDtypeStruct((n_seq * dec, BRANCH_W), F32),
        compiler_params=pltpu.CompilerParams(
            dimension_semantics=("parallel", "arbitrary"), vmem_limit_bytes=VMEM_LIMIT),
        name="moba_dec",
    )(page_table, zs, zs, zs, *([cache] * pages))


def _out_kernel(x_ref, oa_ref, ob_ref, ga_ref, gb_ref, gm_ref, wa_ref, wb_ref, wo_ref, y_ref):
    d_model = x_ref.shape[1]
    ua = (oa_ref[...] * ga_ref[...].astype(F32)).astype(BF16)
    ub = (ob_ref[...] * gb_ref[...].astype(F32)).astype(BF16)
    ya = jnp.dot(ua, wa_ref[...], preferred_element_type=F32)
    yb = jnp.dot(ub, wb_ref[...], preferred_element_type=F32)
    merged = (gm_ref[:, :d_model].astype(F32) * ya + gm_ref[:, d_model:].astype(F32) * yb)
    y_ref[...] = x_ref[...] + jnp.dot(merged.astype(BF16), wo_ref[...],
                                      preferred_element_type=F32)


def _out(x, oa, ob, zb, wa, wb, wo, *, tm):
    rows, d_model = x.shape
    assert 2 * d_model == 4 * BRANCH_W
    const = lambda shape: pl.BlockSpec(shape, lambda m: (0, 0), pipeline_mode=pl.Buffered(1))
    return pl.pallas_call(
        _out_kernel,
        grid=(rows // tm,),
        in_specs=[
            pl.BlockSpec((tm, d_model), lambda m: (m, 0)),
            pl.BlockSpec((tm, BRANCH_W), lambda m: (m, 0)),
            pl.BlockSpec((tm, BRANCH_W), lambda m: (m, 0)),
            pl.BlockSpec((tm, BRANCH_W), lambda m: (m, T_GA)),
            pl.BlockSpec((tm, BRANCH_W), lambda m: (m, T_GB)),
            pl.BlockSpec((tm, 2 * d_model), lambda m: (m, T_GM * BRANCH_W // (2 * d_model))),
            const((BRANCH_W, d_model)),
            const((BRANCH_W, d_model)),
            const((d_model, d_model)),
        ],
        out_specs=pl.BlockSpec((tm, d_model), lambda m: (m, 0)),
        out_shape=jax.ShapeDtypeStruct((rows, d_model), F32),
        compiler_params=pltpu.CompilerParams(
            dimension_semantics=("parallel",), vmem_limit_bytes=VMEM_LIMIT),
        name="out",
    )(x, oa, ob, zb, zb, zb, wa, wb, wo)


def _rope_table(pos):
    pos = pos.astype(F32)[:, None]
    parts = []
    for d in (D_A, HEAD_W):
        half = d // 2
        inv = ROPE_THETA ** (-2.0 * jnp.arange(half, dtype=F32) / d)
        ang = pos * inv[None, :]
        cos, sin = jnp.cos(ang), jnp.sin(ang)
        reps = LANES // d
        parts.append(jnp.tile(jnp.concatenate([cos, cos], axis=-1), (1, reps)))
        parts.append(jnp.tile(jnp.concatenate([-sin, sin], axis=-1), (1, reps)))
    return jnp.concatenate(parts, axis=-1)


def _aux_rows(q_norm_a, k_norm_a, q_norm_b, k_norm_b, b_merge):
    zero = jnp.zeros((BRANCH_W,), F32)
    rows = [jnp.tile(q_norm_a, BRANCH_W // D_A), jnp.tile(k_norm_a, BRANCH_W // D_A), zero, zero,
            jnp.tile(q_norm_b, BRANCH_W // HEAD_W), jnp.tile(k_norm_b, BRANCH_W // HEAD_W),
            zero, zero]
    rows += list(b_merge.reshape(-1, BRANCH_W))
    return jnp.stack(rows)[:, None, :]


def kernel(x_prompt, x_sample, cache_kv_diff, cache_kv_moba, page_table, norm_gain, w_in, q_norm_a, k_norm_a, lambda_q1, lambda_k1, lambda_q2, lambda_k2, subln_gain, q_norm_b, k_norm_b, b_merge, w_up_a, w_up_b, w_out):
    batch, seq, d_model = x_prompt.shape
    n_seq, dec, _ = x_sample.shape
    depth, n_pool, page = cache_kv_diff.shape[:3]
    n_pages = page_table.shape[1]
    past = n_pages * page
    assert w_in.shape[2] == N_TILES * BRANCH_W and seq % MOBA_BLOCK == 0

    cache_d, cache_m = cache_kv_diff, cache_kv_moba
    assert cache_d.shape[3:] == cache_m.shape[3:] == (2, N_HEADS, HEAD_W)
    tab_p = _rope_table(jnp.arange(seq))
    tab_s = jnp.tile(_rope_table(past + jnp.arange(dec)), (n_seq, 1))

    yp = x_prompt.reshape(batch * seq, d_model)
    ys = x_sample.reshape(n_seq * dec, d_model)
    kv_p = kv_s = None
    for l in range(depth):
        lam_init = _lambda_init(l)
        g = norm_gain[l][None, :]
        w_in_l = w_in[l].astype(BF16)
        wa, wb, wo = w_up_a[l].astype(BF16), w_up_b[l].astype(BF16), w_out[l].astype(BF16)
        aux = _aux_rows(q_norm_a[l], k_norm_a[l], q_norm_b[l], k_norm_b[l], b_merge[l])
        lam_params = jnp.stack([lambda_q1[l], lambda_k1[l], lambda_q2[l], lambda_k2[l]])
        sg = subln_gain[l][None, :]

        zb, *kv_p, vt = _proj(yp, g, w_in_l, tab_p, aux, kv_p, layer=l, depth=depth, tm=512,
                              z_dtype=BF16, vt_seq=seq)
        oa = _diff_attn(zb, vt, lam_params, sg, batch=batch, seq=seq, lam_init=lam_init)
        ob = _moba_attn(zb, vt, batch=batch, seq=seq)
        yp = _out(yp, oa, ob, zb, wa, wb, wo, tm=256)

        zs, *kv_s = _proj(ys, g, w_in_l, tab_s, aux, kv_s, layer=l, depth=depth, tm=n_seq * dec,
                          z_dtype=F32)
        oa = _diff_dec(page_table, zs, cache_d, lam_params, sg, layer=l, dec=dec, pages=4,
                       lam_init=lam_init)
        ob = _moba_dec(page_table, zs, cache_m, layer=l, dec=dec, pages=4)
        ys = _out(ys, oa, ob, zs, wa, wb, wo, tm=n_seq * dec)

    kv_shape = lambda rows_a, rows_b: (depth, rows_a, rows_b, 2, N_HEADS, HEAD_W)
    return (yp.reshape(batch, seq, d_model),
            ys.reshape(n_seq, dec, d_model),
            kv_p[0].reshape(kv_shape(batch, seq)),
            kv_p[1].reshape(kv_shape(batch, seq)),
            kv_s[0].reshape(kv_shape(n_seq, dec)),
            kv_s[1].reshape(kv_shape(n_seq, dec)))
```

```python
import functools
import math

import jax
import jax.numpy as jnp
from jax import lax
from jax.experimental import pallas as pl
from jax.experimental.pallas import tpu as pltpu

F32 = jnp.float32
BF16 = jnp.bfloat16

ROPE_THETA = 10000.0
NORM_EPS = 1e-6
N_HEADS = 8
HEAD_W = 128
BRANCH_W = N_HEADS * HEAD_W
D_A = 64
MOBA_BLOCK = 256
MOBA_TOPK = 3
LANES = 128
ATT_TILE = 256
PROJ_SUB = 256
LOG2E = math.log2(math.e)
VT_ONES = 16
VT_ROWS = HEAD_W + VT_ONES
DIFF_HEADS = 2
MOBA_HEADS = 4
KV_ROWS = 2 * N_HEADS
VMEM_LIMIT = 56 * 1024 * 1024

T_QA, T_KA, T_VA, T_GA, T_QB, T_KB, T_VB, T_GB, T_GM = 0, 1, 2, 3, 4, 5, 6, 7, 8
N_TILES = 12

_NT = (((1,), (1,)), ((), ()))


def _lambda_init(layer):
    return 0.8 - 0.6 * math.exp(-0.3 * layer)


def _norm_rope(x, gain, cos, sin, d):
    lane = lax.broadcasted_iota(jnp.int32, x.shape, 1)
    x2 = x * x
    if d == 64:
        lo = lane < 64
        s_lo = jnp.sum(jnp.where(lo, x2, 0.0), axis=-1, keepdims=True)
        s_hi = jnp.sum(jnp.where(lo, 0.0, x2), axis=-1, keepdims=True)
        ms = jnp.where(lo, s_lo, s_hi) * (1.0 / 64.0)
    else:
        ms = jnp.sum(x2, axis=-1, keepdims=True) * (1.0 / 128.0)
    y = x * lax.rsqrt(ms + NORM_EPS) * gain
    if d == 64:
        partner = jnp.where((lane & 63) < 32, pltpu.roll(y, 96, 1), pltpu.roll(y, 32, 1))
    else:
        partner = pltpu.roll(y, 64, 1)
    return y * cos + partner * sin


def _proj_kernel(*refs, emit_vt, aliased):
    x_ref, g_ref, w_ref, tab_ref, aux_ref = refs[:5]
    outs = refs[5 + (2 if aliased else 0):]
    zb_ref, kvd_ref, kvm_ref = outs[:3]
    vt_ref = outs[3] if emit_vt else None
    h_scr = outs[-1]
    n = pl.program_id(1)

    @pl.when(n == 0)
    def _():
        x = x_ref[...]
        ms = jnp.mean(x * x, axis=-1, keepdims=True)
        h_scr[...] = (x * lax.rsqrt(ms + NORM_EPS) * g_ref[...]).astype(BF16)

    aux = aux_ref[0]
    tm = h_scr.shape[0]
    sub = min(tm, PROJ_SUB)

    def row_blocks():
        for r in range(tm // sub):
            rows = slice(r * sub, (r + 1) * sub)
            yield r, rows, jnp.dot(h_scr[rows, :], w_ref[...], preferred_element_type=F32)

    def put_norm_rope(d, q_scale, f32_ref):
        base = 0 if d == 64 else 2 * LANES
        for _, rows, z in row_blocks():
            cos = tab_ref[rows, base:base + LANES]
            sin = tab_ref[rows, base + LANES:base + 2 * LANES]
            for c in range(N_HEADS):
                sl = slice(c * HEAD_W, (c + 1) * HEAD_W)
                r = _norm_rope(z[:, sl], aux[:, sl], cos, sin, d)
                if f32_ref is not None:
                    f32_ref[rows, c, :] = r
                zb_ref[rows, sl] = (r * q_scale).astype(zb_ref.dtype)

    def put_values(f32_ref):
        for r, rows, z in row_blocks():
            for c in range(N_HEADS):
                zc = z[:, c * HEAD_W:(c + 1) * HEAD_W]
                f32_ref[rows, c, :] = zc
                if emit_vt:
                    for u in range(sub // ATT_TILE):
                        chunk = r * (sub // ATT_TILE) + u
                        vt_ref[c, chunk, :HEAD_W, :] = (
                            zc[u * ATT_TILE:(u + 1) * ATT_TILE, :].T.astype(BF16))
                        vt_ref[c, chunk, HEAD_W:, :] = jnp.ones((VT_ONES, ATT_TILE), BF16)
            zb_ref[rows, :] = z.astype(zb_ref.dtype)

    def put_silu():
        for _, rows, z in row_blocks():
            zb_ref[rows, :] = (z * jax.nn.sigmoid(z)).astype(zb_ref.dtype)

    def put_sigmoid():
        for _, rows, z in row_blocks():
            zb_ref[rows, :] = jax.nn.sigmoid(z + aux).astype(zb_ref.dtype)

    pl.when(n == T_QA)(lambda: put_norm_rope(64, D_A ** -0.5 * LOG2E, None))
    pl.when(n == T_KA)(lambda: put_norm_rope(64, 1.0, kvd_ref))
    pl.when(n == T_VA)(lambda: put_values(kvd_ref))
    pl.when(n == T_QB)(lambda: put_norm_rope(128, 1.0, None))
    pl.when(n == T_KB)(lambda: put_norm_rope(128, 1.0, kvm_ref))
    pl.when(n == T_VB)(lambda: put_values(kvm_ref))
    pl.when((n == T_GA) | (n == T_GB))(put_silu)
    pl.when(n >= T_GM)(put_sigmoid)


def _proj(x, g, w_bf, tab, aux, kv_bufs, *, layer, depth, tm, z_dtype, vt_seq=None):
    rows, d_model = x.shape
    tab_blocks = tab.shape[0] // tm
    emit_vt = vt_seq is not None
    aliased = kv_bufs is not None
    clip01 = lambda v: jnp.minimum(jnp.maximum(v, 0), 1)
    in_specs = [
        pl.BlockSpec((tm, d_model), lambda m, n: (m, 0)),
        pl.BlockSpec((1, d_model), lambda m, n: (0, 0)),
        pl.BlockSpec((d_model, BRANCH_W), lambda m, n: (0, n)),
        pl.BlockSpec((tm, 4 * LANES), lambda m, n: (m % tab_blocks, 0)),
        pl.BlockSpec((1, 1, BRANCH_W), lambda m, n: (n, 0, 0)),
    ]
    args = [x, g, w_bf, tab, aux]
    if aliased:
        in_specs += [pl.BlockSpec(memory_space=pl.ANY)] * 2
        args += list(kv_bufs)
    kv_sds = jax.ShapeDtypeStruct((depth, rows, KV_ROWS, HEAD_W), F32)
    out_specs = [
        pl.BlockSpec((tm, BRANCH_W), lambda m, n: (m, n)),
        pl.BlockSpec((None, tm, N_HEADS, HEAD_W), lambda m, n: (layer, m, clip01(n - T_KA), 0)),
        pl.BlockSpec((None, tm, N_HEADS, HEAD_W), lambda m, n: (layer, m, clip01(n - T_KB), 0)),
    ]
    out_shape = [jax.ShapeDtypeStruct((rows, N_TILES * BRANCH_W), z_dtype), kv_sds, kv_sds]
    if emit_vt:
        assert vt_seq % tm == 0 and tm % ATT_TILE == 0
        mpb = vt_seq // tm
        out_specs.append(pl.BlockSpec(
            (None, N_HEADS, tm // ATT_TILE, VT_ROWS, ATT_TILE),
            lambda m, n: (m // mpb, jnp.where(n >= T_VB, 1, 0), m % mpb, 0, 0)))
        out_shape.append(jax.ShapeDtypeStruct(
            (rows // vt_seq, 2 * N_HEADS, vt_seq // ATT_TILE, VT_ROWS, ATT_TILE), BF16))
    return pl.pallas_call(
        functools.partial(_proj_kernel, emit_vt=emit_vt, aliased=aliased),
        grid=(rows // tm, N_TILES),
        in_specs=in_specs,
        out_specs=out_specs,
        out_shape=out_shape,
        input_output_aliases={5: 1, 6: 2} if aliased else {},
        scratch_shapes=[pltpu.VMEM((tm, d_model), BF16)],
        compiler_params=pltpu.CompilerParams(
            dimension_semantics=("parallel", "arbitrary"), vmem_limit_bytes=VMEM_LIMIT),
        name="proj",
    )(*args)


def _lambda_full(lp_ref, lam_init):
    a = jnp.sum(lp_ref[0:1, :] * lp_ref[1:2, :], axis=-1, keepdims=True)
    b = jnp.sum(lp_ref[2:3, :] * lp_ref[3:4, :], axis=-1, keepdims=True)
    return jnp.exp(a) - jnp.exp(b) + lam_init


def _subln(o, gain, lam_init):
    ms = jnp.mean(o * o, axis=-1, keepdims=True)
    return o * lax.rsqrt(ms + NORM_EPS) * gain * (1.0 - lam_init)


def _flash_heads(qi, score_diag, score_past, vt_ref, s_scr, p_scr, acc_scr):
    heads = range(s_scr.shape[1])
    width = s_scr.shape[-1]
    ones_row = jnp.ones((1, width), F32)

    def chunk_at(pos):
        return jnp.where(pos == 0, qi, pos - 1)

    def p_times_v(g, pos, slot):
        return jnp.dot(vt_ref[g, chunk_at(pos)], p_scr[slot, g], preferred_element_type=F32)

    def softmax_stage(slot, g, m, cm, keep):
        m_new = jnp.maximum(m, cm)
        a = jnp.exp2(m - m_new)
        m_sub = jnp.where(keep > 0.5, m_new, jnp.inf)
        for c in range(width // LANES):
            sl = slice(c * LANES, (c + 1) * LANES)
            p_scr[slot, g, :, sl] = jnp.exp2(s_scr[slot, g, :, sl] - m_sub[:, sl]).astype(BF16)
        return m_new, a

    def qk_stage(slot, score_next):
        scored = [score_next(g) for g in heads]
        cms, keeps = [], []
        for g, (s, keep) in enumerate(scored):
            s_scr[slot, g] = s
            cm = jnp.max(s, axis=0, keepdims=True)
            keep = ones_row if keep is None else jnp.where(keep, 1.0, 0.0)
            cms.append(jnp.where(keep > 0.5, cm, -jnp.inf))
            keeps.append(keep)
        return tuple(cms), tuple(keeps)

    def step(pos, slot, carry, score_next):
        ms, cms, keeps = carry
        other = 1 - slot
        nxt = qk_stage(other, score_next) if score_next is not None else (cms, keeps)
        pvs = [p_times_v(g, jnp.maximum(pos - 1, 0), other) for g in heads]
        new_m = []
        for g in heads:
            m_new, a = softmax_stage(slot, g, ms[g], cms[g], keeps[g])
            acc_scr[g] = a * (acc_scr[g] + pvs[g])
            new_m.append(m_new)
        return (tuple(new_m),) + nxt

    for g in heads:
        p_scr[1, g] = jnp.zeros(p_scr.shape[2:], BF16)
        acc_scr[g] = jnp.zeros(acc_scr.shape[1:], F32)
    carry = (tuple(jnp.full((1, width), -jnp.inf, F32) for _ in heads),) + qk_stage(
        0, lambda g: (score_diag(g), None))

    def pair(u, carry):
        pos = 2 * u
        carry = step(pos, 0, carry, lambda g: score_past(g, pos))
        return step(pos + 1, 1, carry, lambda g: score_past(g, pos + 1))

    carry = lax.fori_loop(0, qi // 2, pair, carry)
    carry = lax.cond(qi % 2 == 1,
                     lambda c: step(qi - 1, 0, c, lambda g: score_past(g, qi - 1)),
                     lambda c: c, carry)
    step(qi, qi % 2, carry, None)
    return [acc_scr[g] + p_times_v(g, qi, qi % 2) for g in heads]


def _top_blocks(gate, n_valid, axis):
    idx = lax.broadcasted_iota(jnp.int32, gate.shape, axis)
    size = gate.shape[axis]
    g = jnp.where(idx < n_valid, gate, -jnp.inf)
    sel = jnp.zeros(gate.shape, F32)
    for r in range(MOBA_TOPK):
        mx = jnp.max(g, axis=axis, keepdims=True)
        pick = jnp.min(jnp.where(g == mx, idx, size), axis=axis, keepdims=True)
        hit = idx == pick
        sel = jnp.maximum(sel, jnp.where(hit, jnp.where(r < n_valid, 1.0, 0.0), 0.0))
        g = jnp.where(hit, -jnp.inf, g)
    return sel


def _head_cols(g):
    return slice(g * HEAD_W, (g + 1) * HEAD_W)


def _diff_attn_kernel(q_ref, k_ref, vt_ref, lp_ref, sg_ref, o_ref, s_scr, p_scr, acc_scr, *,
                      lam_init):
    qi = pl.program_id(2)
    tq = ATT_TILE
    key = lax.broadcasted_iota(jnp.int32, (tq, 2 * tq), 0)
    qcol = lax.broadcasted_iota(jnp.int32, (tq, 2 * tq), 1) % tq
    lane = lax.broadcasted_iota(jnp.int32, (tq, HEAD_W), 1)

    q12 = []
    for g in range(s_scr.shape[1]):
        q = q_ref[:, _head_cols(g)]
        zero = jnp.zeros_like(q)
        q12.append(jnp.concatenate([jnp.where(lane < D_A, q, zero),
                                    jnp.where(lane < D_A, zero, q)], axis=0))

    def scores(g, j):
        off = pl.multiple_of(j * tq, tq)
        return lax.dot_general(k_ref[pl.ds(off, tq), _head_cols(g)], q12[g], _NT,
                               preferred_element_type=F32)

    outs = _flash_heads(qi, lambda g: jnp.where(key <= qcol, scores(g, qi), -jnp.inf),
                        lambda g, j: (scores(g, j), None), vt_ref, s_scr, p_scr, acc_scr)
    lam = _lambda_full(lp_ref, lam_init)
    for g, acc in enumerate(outs):
        o = acc[:HEAD_W, :] / acc[HEAD_W:HEAD_W + 1, :]
        d = (o[:, :tq] - lam * o[:, tq:]).T
        o_ref[:, _head_cols(g)] = _subln(d, sg_ref[...], lam_init)


def _attn_specs(branch, seq, nq, nh):
    q_tile, k_tile = (T_QA, T_KA) if branch == 0 else (T_QB, T_KB)
    groups = N_HEADS // nh
    width = nh * HEAD_W
    return [
        pl.BlockSpec((ATT_TILE, width), lambda b, h, i: (b * nq + i, q_tile * groups + h)),
        pl.BlockSpec((seq, width), lambda b, h, i: (b, k_tile * groups + h)),
        pl.BlockSpec((None, nh, nq, VT_ROWS, ATT_TILE),
                     lambda b, h, i: (b, branch * groups + h, 0, 0, 0)),
    ]


def _flash_scratch(width, nh):
    return [pltpu.VMEM((2, nh, ATT_TILE, width), F32),
            pltpu.VMEM((2, nh, ATT_TILE, width), BF16),
            pltpu.VMEM((nh, VT_ROWS, width), F32)]


def _attn_call(kern, name, branch, nh, extra_specs, scratch, args, *, batch, seq):
    nq = seq // ATT_TILE
    width = nh * HEAD_W
    return pl.pallas_call(
        kern,
        grid=(batch, N_HEADS // nh, nq),
        in_specs=_attn_specs(branch, seq, nq, nh) + extra_specs,
        out_specs=pl.BlockSpec((ATT_TILE, width), lambda b, h, i: (b * nq + i, h)),
        out_shape=jax.ShapeDtypeStruct((batch * seq, BRANCH_W), F32),
        scratch_shapes=scratch,
        compiler_params=pltpu.CompilerParams(
            dimension_semantics=("parallel", "parallel", "arbitrary"),
            vmem_limit_bytes=VMEM_LIMIT),
        name=name,
    )(*args)


def _diff_attn(zb, vt, lam_params, subln_g, *, batch, seq, lam_init):
    kern = functools.partial(_diff_attn_kernel, lam_init=lam_init)
    extra = [pl.BlockSpec((4, D_A), lambda b, h, i: (0, 0)),
             pl.BlockSpec((1, HEAD_W), lambda b, h, i: (0, 0))]
    return _attn_call(kern, "diff_attn", 0, DIFF_HEADS, extra,
                      _flash_scratch(2 * ATT_TILE, DIFF_HEADS),
                      (zb, zb, vt, lam_params, subln_g), batch=batch, seq=seq)


def _moba_attn_kernel(q_ref, k_ref, vt_ref, o_ref, kmean_scr, sel_scr, s_scr, p_scr, acc_scr, *,
                      nb):
    qi = pl.program_id(2)
    tq = ATT_TILE
    scale = HEAD_W ** -0.5 * LOG2E
    heads = range(s_scr.shape[1])

    @pl.when(qi == 0)
    def _():
        kmean_scr[...] = jnp.zeros(kmean_scr.shape, F32)
        for g in heads:
            for j in range(nb):
                blk = k_ref[j * tq:(j + 1) * tq, _head_cols(g)].astype(F32)
                kmean_scr[g, j:j + 1, :] = jnp.mean(blk, axis=0, keepdims=True)

    key = lax.broadcasted_iota(jnp.int32, (tq, tq), 0)
    qcol = lax.broadcasted_iota(jnp.int32, (tq, tq), 1)
    qs = [q_ref[:, _head_cols(g)] for g in heads]

    def scores(g, j):
        off = pl.multiple_of(j * tq, tq)
        return lax.dot_general(k_ref[pl.ds(off, tq), _head_cols(g)], qs[g], _NT,
                               preferred_element_type=F32) * scale

    for g in heads:
        gate = lax.dot_general(kmean_scr[g], qs[g].astype(F32), _NT,
                               precision=lax.Precision.HIGHEST, preferred_element_type=F32)
        sel_scr[g] = _top_blocks(gate, qi, axis=0)

    def score_past(g, j):
        return scores(g, j), sel_scr[g, pl.ds(j, 1), :] > 0.5

    outs = _flash_heads(qi, lambda g: jnp.where(key <= qcol, scores(g, qi), -jnp.inf), score_past,
                        vt_ref, s_scr, p_scr, acc_scr)
    for g, acc in enumerate(outs):
        o_ref[:, _head_cols(g)] = (acc[:HEAD_W, :] / acc[HEAD_W:HEAD_W + 1, :]).T


def _moba_attn(zb, vt, *, batch, seq):
    assert ATT_TILE == MOBA_BLOCK
    nq = seq // ATT_TILE
    nb_pad = -(-nq // 8) * 8
    kern = functools.partial(_moba_attn_kernel, nb=nq)
    scratch = [pltpu.VMEM((MOBA_HEADS, nb_pad, HEAD_W), F32),
               pltpu.VMEM((MOBA_HEADS, nb_pad, ATT_TILE), F32)] + _flash_scratch(ATT_TILE, MOBA_HEADS)
    return _attn_call(kern, "moba_attn", 1, MOBA_HEADS, [], scratch, (zb, zb, vt),
                      batch=batch, seq=seq)


Q_ROWS = 16


def _state_rows(h):
    return slice(Q_ROWS * h, Q_ROWS * (h + 1))


def _cached_rows(kv_refs, gs, kv):
    return jnp.concatenate(
        [kv_refs[g][:, kv].reshape(-1, HEAD_W).astype(BF16) for g in gs], axis=0)


def _own_head_mask():
    row_head = lax.broadcasted_iota(jnp.int32, (N_HEADS * Q_ROWS, LANES), 0) // Q_ROWS
    col_head = lax.broadcasted_iota(jnp.int32, (N_HEADS * Q_ROWS, LANES), 1) % N_HEADS
    return row_head == col_head


def _mask_cols(y, mask, fill):
    return jnp.concatenate(
        [jnp.where(mask, y[:, t * LANES:(t + 1) * LANES], fill) for t in range(y.shape[1] // LANES)],
        axis=1)


def _own_p_times_v(pb, vn_ref):
    return jnp.concatenate(
        [jnp.dot(pb[_state_rows(h), :], _pad_rows(vn_ref[:, _head_cols(h)]).astype(BF16),
                 preferred_element_type=F32) for h in range(N_HEADS)], axis=0)


def _pad_rows(x_f32):
    return jnp.concatenate([x_f32, jnp.zeros((LANES - x_f32.shape[0], x_f32.shape[1]), F32)], axis=0)


def _head_queries(q_ref, h, halves, q_scale=1.0):
    q = q_ref[:, HEAD_W * h:HEAD_W * (h + 1)]
    zero = jnp.zeros_like(q)
    if halves:
        lane = lax.broadcasted_iota(jnp.int32, q.shape, 1)
        q16 = jnp.concatenate([jnp.where(lane < D_A, q, zero), jnp.where(lane < D_A, zero, q)], axis=0)
    else:
        q16 = jnp.concatenate([q * q_scale, zero], axis=0)
    return q16.astype(BF16)


def _own_scores(q16, kn_ref, h, dec):
    kn = _pad_rows(kn_ref[:, HEAD_W * h:HEAD_W * (h + 1)]).astype(BF16)
    s = lax.dot_general(q16, kn, _NT, preferred_element_type=F32)
    key = lax.broadcasted_iota(jnp.int32, s.shape, 1)
    t = lax.broadcasted_iota(jnp.int32, s.shape, 0) % dec
    return s, key <= t


def _diff_dec_kernel(pt_ref, q_ref, kn_ref, vn_ref, lp_ref, sg_ref, *rest, pages, lam_init):
    kv_refs = rest[:pages]
    o_ref = rest[pages]
    q_scr, m_scr, l_scr, acc_scr = rest[pages + 1:]
    c = pl.program_id(1)
    dec = q_ref.shape[0]

    @pl.when(c == 0)
    def _():
        for h in range(N_HEADS):
            q_scr[_state_rows(h), :] = _head_queries(q_ref, h, halves=True)
        m_scr[...] = jnp.full(m_scr.shape, -jnp.inf, F32)
        l_scr[...] = jnp.zeros(l_scr.shape, F32)
        acc_scr[...] = jnp.zeros(acc_scr.shape, F32)

    def update(s, p_times_v):
        m_old = m_scr[...]
        m_new = jnp.maximum(m_old, jnp.max(s, axis=1, keepdims=True))
        alpha = jnp.exp2(m_old - m_new)
        p = jnp.exp2(s - m_new[:, :1])
        l_scr[...] = alpha * l_scr[...] + jnp.sum(p, axis=1, keepdims=True)
        m_scr[...] = m_new
        acc_scr[...] = alpha * acc_scr[...] + p_times_v(p.astype(BF16))

    halves = [range(0, pages // 2), range(pages // 2, pages)]
    own_head = _own_head_mask()
    ys = [lax.dot_general(q_scr[...], _cached_rows(kv_refs, gs, 0), _NT,
                          preferred_element_type=F32) for gs in halves]
    for gs, y in zip(halves, ys):
        v_all = _cached_rows(kv_refs, gs, 1)
        update(_mask_cols(y, own_head, -jnp.inf),
               lambda pb: jnp.dot(pb, v_all, preferred_element_type=F32))

    @pl.when(c == pl.num_programs(1) - 1)
    def _():
        owns = [_own_scores(q_scr[_state_rows(h), :], kn_ref, h, dec) for h in range(N_HEADS)]
        s = jnp.concatenate([jnp.where(ok, s_h, -jnp.inf) for s_h, ok in owns], axis=0)
        update(s, lambda pb: _own_p_times_v(pb, vn_ref))
        o = acc_scr[...] / l_scr[...]
        lam = _lambda_full(lp_ref, lam_init)
        for h in range(N_HEADS):
            r = Q_ROWS * h
            d = o[r:r + dec, :] - lam * o[r + 8:r + 8 + dec, :]
            o_ref[:, _head_cols(h)] = _subln(d, sg_ref[...], lam_init)


def _dec_specs(dec, page, pages, layer, tiles):
    row_specs = [pl.BlockSpec((dec, BRANCH_W), lambda b, c, pt, tile=tile: (b, tile))
                 for tile in tiles]
    cache_specs = [
        pl.BlockSpec((None, None, page, 2, N_HEADS, HEAD_W),
                     lambda b, c, pt, g=g: (layer, pt[b, c * pages + g], 0, 0, 0, 0))
        for g in range(pages)]
    return row_specs, cache_specs


def _diff_dec(page_table, zs, cache, lam_params, subln_g, *, layer, dec, pages, lam_init):
    n_seq, n_pages = page_table.shape
    page = cache.shape[2]
    assert dec == 8 and n_pages % pages == 0 and pages % 2 == 0
    kern = functools.partial(_diff_dec_kernel, pages=pages, lam_init=lam_init)
    row_specs, cache_specs = _dec_specs(dec, page, pages, layer, (T_QA, T_KA, T_VA))
    state = pltpu.VMEM((N_HEADS * Q_ROWS, LANES), F32)
    return pl.pallas_call(
        kern,
        grid_spec=pltpu.PrefetchScalarGridSpec(
            num_scalar_prefetch=1,
            grid=(n_seq, n_pages // pages),
            in_specs=row_specs + [pl.BlockSpec((4, D_A), lambda b, c, pt: (0, 0)),
                                  pl.BlockSpec((1, HEAD_W), lambda b, c, pt: (0, 0))] + cache_specs,
            out_specs=pl.BlockSpec((dec, BRANCH_W), lambda b, c, pt: (b, 0)),
            scratch_shapes=[pltpu.VMEM((N_HEADS * Q_ROWS, HEAD_W), BF16), state, state, state]),
        out_shape=jax.ShapeDtypeStruct((n_seq * dec, BRANCH_W), F32),
        compiler_params=pltpu.CompilerParams(
            dimension_semantics=("parallel", "arbitrary"), vmem_limit_bytes=VMEM_LIMIT),
        name="diff_dec",
    )(page_table, zs, zs, zs, lam_params, subln_g, *([cache] * pages))


def _moba_dec_kernel(pt_ref, q_ref, kn_ref, vn_ref, *rest, pages, page, n_blocks):
    kv_refs = rest[:pages]
    o_ref = rest[pages]
    q_scr, gate_scr, m_scr, l_scr, acc_scr = rest[pages + 1:]
    c = pl.program_id(1)
    dec = q_ref.shape[0]
    scale = HEAD_W ** -0.5 * LOG2E
    pages_per_block = MOBA_BLOCK // page
    blocks_per_step = pages // pages_per_block
    blk_lane = lax.broadcasted_iota(jnp.int32, (N_HEADS * Q_ROWS, LANES), 1)

    @pl.when(c == 0)
    def _():
        for h in range(N_HEADS):
            q_scr[_state_rows(h), :] = _head_queries(q_ref, h, halves=False, q_scale=scale)
        gate_scr[...] = jnp.zeros(gate_scr.shape, F32)
        m_scr[...] = jnp.zeros(m_scr.shape, F32)
        l_scr[...] = jnp.zeros(l_scr.shape, F32)

    def block_softmax(s, p_times_v):
        m = jnp.max(s, axis=1, keepdims=True)
        p = jnp.exp2(s - m)
        return m, jnp.sum(p, axis=1, keepdims=True), p_times_v(p.astype(BF16))

    def block_pages(blk):
        return range(blk * pages_per_block, (blk + 1) * pages_per_block)

    own_head = _own_head_mask()
    ys = [lax.dot_general(q_scr[...], _cached_rows(kv_refs, block_pages(blk), 0), _NT,
                          preferred_element_type=F32) for blk in range(blocks_per_step)]
    for blk in range(blocks_per_step):
        j = c * blocks_per_step + blk
        here = blk_lane == j
        gate = jnp.sum(_mask_cols(ys[blk], own_head, 0.0), axis=1, keepdims=True) * (
            1.0 / (MOBA_BLOCK * scale))
        v_all = _cached_rows(kv_refs, block_pages(blk), 1)
        m, l, acc = block_softmax(_mask_cols(ys[blk], own_head, -jnp.inf),
                                  lambda pb: jnp.dot(pb, v_all, preferred_element_type=F32))
        gate_scr[...] = jnp.where(here, gate, gate_scr[...])
        m_scr[...] = jnp.where(here, m, m_scr[...])
        l_scr[...] = jnp.where(here, l, l_scr[...])
        acc_scr[j] = acc

    @pl.when(c == pl.num_programs(1) - 1)
    def _():
        owns = [_own_scores(q_scr[_state_rows(h), :], kn_ref, h, dec) for h in range(N_HEADS)]
        s = jnp.concatenate([jnp.where(ok, s_h, -jnp.inf) for s_h, ok in owns], axis=0)
        m_own, l_own, acc_own = block_softmax(s, lambda pb: _own_p_times_v(pb, vn_ref))
        picked = _top_blocks(gate_scr[...], n_blocks, axis=1) > 0.5
        m_all = m_scr[...]
        m_fin = jnp.maximum(m_own, jnp.max(jnp.where(picked, m_all, -jnp.inf),
                                           axis=1, keepdims=True))
        w = jnp.where(picked, jnp.exp2(m_all - m_fin), 0.0)
        w_own = jnp.exp2(m_own - m_fin)
        l_fin = w_own * l_own + jnp.sum(w * l_scr[...], axis=1, keepdims=True)
        acc = w_own * acc_own
        for jb in range(n_blocks):
            acc = acc + w[:, jb:jb + 1] * acc_scr[jb]
        o = acc / l_fin
        for h in range(N_HEADS):
            o_ref[:, _head_cols(h)] = o[Q_ROWS * h:Q_ROWS * h + dec, :]


def _moba_dec(page_table, zs, cache, *, layer, dec, pages):
    n_seq, n_pages = page_table.shape
    page = cache.shape[2]
    past = n_pages * page
    assert dec == 8 and past % MOBA_BLOCK == 0 and MOBA_BLOCK % page == 0
    assert pages % (MOBA_BLOCK // page) == 0 and n_pages % pages == 0
    n_blocks = past // MOBA_BLOCK
    assert MOBA_TOPK <= n_blocks <= LANES
    kern = functools.partial(_moba_dec_kernel, pages=pages, page=page, n_blocks=n_blocks)
    row_specs, cache_specs = _dec_specs(dec, page, pages, layer, (T_QB, T_KB, T_VB))
    stat = pltpu.VMEM((N_HEADS * Q_ROWS, LANES), F32)
    return pl.pallas_call(
        kern,
        grid_spec=pltpu.PrefetchScalarGridSpec(
            num_scalar_prefetch=1,
            grid=(n_seq, n_pages // pages),
            in_specs=row_specs + cache_specs,
            out_specs=pl.BlockSpec((dec, BRANCH_W), lambda b, c, pt: (b, 0)),
            scratch_shapes=[pltpu.VMEM((N_HEADS * Q_ROWS, HEAD_W), BF16), stat, stat, stat,
                            pltpu.VMEM((n_blocks, N_HEADS * Q_ROWS, HEAD_W), F32)]),
        out_shape=jax.ShapeDtypeStruct((n_seq * dec, BRANCH_W), F32),
        compiler_params=pltpu.CompilerParams(
            dimension_semantics=("parallel", "arbitrary"), vmem_limit_bytes=VMEM_LIMIT),
        name="moba_dec",
    )(page_table, zs, zs, zs, *([cache] * pages))


def _out_kernel(x_ref, oa_ref, ob_ref, ga_ref, gb_ref, gm_ref, wa_ref, wb_ref, wo_ref, y_ref):
    d_model = x_ref.shape[1]
    ua = (oa_ref[...] * ga_ref[...].astype(F32)).astype(BF16)
    ub = (ob_ref[...] * gb_ref[...].astype(F32)).astype(BF16)
    ya = jnp.dot(ua, wa_ref[...], preferred_element_type=F32)
    yb = jnp.dot(ub, wb_ref[...], preferred_element_type=F32)
    merged = (gm_ref[:, :d_model].astype(F32) * ya + gm_ref[:, d_model:].astype(F32) * yb)
    y_ref[...] = x_ref[...] + jnp.dot(merged.astype(BF16), wo_ref[...],
                                      preferred_element_type=F32)


def _out(x, oa, ob, zb, wa, wb, wo, *, tm):
    rows, d_model = x.shape
    assert 2 * d_model == 4 * BRANCH_W
    const = lambda shape: pl.BlockSpec(shape, lambda m: (0, 0), pipeline_mode=pl.Buffered(1))
    return pl.pallas_call(
        _out_kernel,
        grid=(rows // tm,),
        in_specs=[
            pl.BlockSpec((tm, d_model), lambda m: (m, 0)),
            pl.BlockSpec((tm, BRANCH_W), lambda m: (m, 0)),
            pl.BlockSpec((tm, BRANCH_W), lambda m: (m, 0)),
            pl.BlockSpec((tm, BRANCH_W), lambda m: (m, T_GA)),
            pl.BlockSpec((tm, BRANCH_W), lambda m: (m, T_GB)),
            pl.BlockSpec((tm, 2 * d_model), lambda m: (m, T_GM * BRANCH_W // (2 * d_model))),
            const((BRANCH_W, d_model)),
            const((BRANCH_W, d_model)),
            const((d_model, d_model)),
        ],
        out_specs=pl.BlockSpec((tm, d_model), lambda m: (m, 0)),
        out_shape=jax.ShapeDtypeStruct((rows, d_model), F32),
        compiler_params=pltpu.CompilerParams(
            dimension_semantics=("parallel",), vmem_limit_bytes=VMEM_LIMIT),
        name="out",
    )(x, oa, ob, zb, zb, zb, wa, wb, wo)


def _rope_table(pos):
    pos = pos.astype(F32)[:, None]
    parts = []
    for d in (D_A, HEAD_W):
        half = d // 2
        inv = ROPE_THETA ** (-2.0 * jnp.arange(half, dtype=F32) / d)
        ang = pos * inv[None, :]
        cos, sin = jnp.cos(ang), jnp.sin(ang)
        reps = LANES // d
        parts.append(jnp.tile(jnp.concatenate([cos, cos], axis=-1), (1, reps)))
        parts.append(jnp.tile(jnp.concatenate([-sin, sin], axis=-1), (1, reps)))
    return jnp.concatenate(parts, axis=-1)


def _aux_rows(q_norm_a, k_norm_a, q_norm_b, k_norm_b, b_merge):
    zero = jnp.zeros((BRANCH_W,), F32)
    rows = [jnp.tile(q_norm_a, BRANCH_W // D_A), jnp.tile(k_norm_a, BRANCH_W // D_A), zero, zero,
            jnp.tile(q_norm_b, BRANCH_W // HEAD_W), jnp.tile(k_norm_b, BRANCH_W // HEAD_W),
            zero, zero]
    rows += list(b_merge.reshape(-1, BRANCH_W))
    return jnp.stack(rows)[:, None, :]


def kernel(x_prompt, x_sample, cache_kv_diff, cache_kv_moba, page_table, norm_gain, w_in, q_norm_a, k_norm_a, lambda_q1, lambda_k1, lambda_q2, lambda_k2, subln_gain, q_norm_b, k_norm_b, b_merge, w_up_a, w_up_b, w_out):
    batch, seq, d_model = x_prompt.shape
    n_seq, dec, _ = x_sample.shape
    depth, n_pool, page = cache_kv_diff.shape[:3]
    n_pages = page_table.shape[1]
    past = n_pages * page
    assert w_in.shape[2] == N_TILES * BRANCH_W and seq % MOBA_BLOCK == 0

    cache_d, cache_m = cache_kv_diff, cache_kv_moba
    assert cache_d.shape[3:] == cache_m.shape[3:] == (2, N_HEADS, HEAD_W)
    tab_p = _rope_table(jnp.arange(seq))
    tab_s = jnp.tile(_rope_table(past + jnp.arange(dec)), (n_seq, 1))

    yp = x_prompt.reshape(batch * seq, d_model)
    ys = x_sample.reshape(n_seq * dec, d_model)
    kv_p = kv_s = None
    for l in range(depth):
        lam_init = _lambda_init(l)
        g = norm_gain[l][None, :]
        w_in_l = w_in[l].astype(BF16)
        wa, wb, wo = w_up_a[l].astype(BF16), w_up_b[l].astype(BF16), w_out[l].astype(BF16)
        aux = _aux_rows(q_norm_a[l], k_norm_a[l], q_norm_b[l], k_norm_b[l], b_merge[l])
        lam_params = jnp.stack([lambda_q1[l], lambda_k1[l], lambda_q2[l], lambda_k2[l]])
        sg = subln_gain[l][None, :]

        zb, *kv_p, vt = _proj(yp, g, w_in_l, tab_p, aux, kv_p, layer=l, depth=depth, tm=512,
                              z_dtype=BF16, vt_seq=seq)
        oa = _diff_attn(zb, vt, lam_params, sg, batch=batch, seq=seq, lam_init=lam_init)
        ob = _moba_attn(zb, vt, batch=batch, seq=seq)
        yp = _out(yp, oa, ob, zb, wa, wb, wo, tm=256)

        zs, *kv_s = _proj(ys, g, w_in_l, tab_s, aux, kv_s, layer=l, depth=depth, tm=n_seq * dec,
                          z_dtype=F32)
        oa = _diff_dec(page_table, zs, cache_d, lam_params, sg, layer=l, dec=dec, pages=8,
                       lam_init=lam_init)
        ob = _moba_dec(page_table, zs, cache_m, layer=l, dec=dec, pages=8)
        ys = _out(ys, oa, ob, zs, wa, wb, wo, tm=n_seq * dec)

    kv_shape = lambda rows_a, rows_b: (depth, rows_a, rows_b, 2, N_HEADS, HEAD_W)
    return (yp.reshape(batch, seq, d_model),
            ys.reshape(n_seq, dec, d_model),
            kv_p[0].reshape(kv_shape(batch, seq)),
            kv_p[1].reshape(kv_shape(batch, seq)),
            kv_s[0].reshape(kv_shape(n_seq, dec)),
            kv_s[1].reshape(kv_shape(n_seq, dec)))
```

```python
import functools
import math

import jax
import jax.numpy as jnp
from jax import lax
from jax.experimental import pallas as pl
from jax.experimental.pallas import tpu as pltpu

F32 = jnp.float32
BF16 = jnp.bfloat16

ROPE_THETA = 10000.0
NORM_EPS = 1e-6
N_HEADS = 8
HEAD_W = 128
BRANCH_W = N_HEADS * HEAD_W
D_A = 64
MOBA_BLOCK = 256
MOBA_TOPK = 3
LANES = 128
ATT_TILE = 256
PROJ_SUB = 256
LOG2E = math.log2(math.e)
VT_ONES = 16
VT_ROWS = HEAD_W + VT_ONES
DIFF_HEADS = 4
MOBA_HEADS = 4
MXU_GROUP_COLS = 1024
KV_ROWS = 2 * N_HEADS
VMEM_LIMIT = 56 * 1024 * 1024

T_QA, T_KA, T_VA, T_GA, T_QB, T_KB, T_VB, T_GB, T_GM = 0, 1, 2, 3, 4, 5, 6, 7, 8
N_TILES = 12

_NT = (((1,), (1,)), ((), ()))


def _lambda_init(layer):
    return 0.8 - 0.6 * math.exp(-0.3 * layer)


def _norm_rope(x, gain, cos, sin, d):
    lane = lax.broadcasted_iota(jnp.int32, x.shape, 1)
    x2 = x * x
    if d == 64:
        lo = lane < 64
        s_lo = jnp.sum(jnp.where(lo, x2, 0.0), axis=-1, keepdims=True)
        s_hi = jnp.sum(jnp.where(lo, 0.0, x2), axis=-1, keepdims=True)
        ms = jnp.where(lo, s_lo, s_hi) * (1.0 / 64.0)
    else:
        ms = jnp.sum(x2, axis=-1, keepdims=True) * (1.0 / 128.0)
    y = x * lax.rsqrt(ms + NORM_EPS) * gain
    if d == 64:
        partner = jnp.where((lane & 63) < 32, pltpu.roll(y, 96, 1), pltpu.roll(y, 32, 1))
    else:
        partner = pltpu.roll(y, 64, 1)
    return y * cos + partner * sin


def _proj_kernel(*refs, emit_vt, aliased):
    x_ref, g_ref, w_ref, tab_ref, aux_ref = refs[:5]
    outs = refs[5 + (2 if aliased else 0):]
    zb_ref, kvd_ref, kvm_ref = outs[:3]
    vt_ref = outs[3] if emit_vt else None
    h_scr = outs[-1]
    n = pl.program_id(1)

    @pl.when(n == 0)
    def _():
        x = x_ref[...]
        ms = jnp.mean(x * x, axis=-1, keepdims=True)
        h_scr[...] = (x * lax.rsqrt(ms + NORM_EPS) * g_ref[...]).astype(BF16)

    aux = aux_ref[0]
    tm = h_scr.shape[0]
    sub = min(tm, PROJ_SUB)

    def row_blocks():
        for r in range(tm // sub):
            rows = slice(r * sub, (r + 1) * sub)
            yield r, rows, jnp.dot(h_scr[rows, :], w_ref[...], preferred_element_type=F32)

    def put_norm_rope(d, q_scale, f32_ref):
        base = 0 if d == 64 else 2 * LANES
        for _, rows, z in row_blocks():
            cos = tab_ref[rows, base:base + LANES]
            sin = tab_ref[rows, base + LANES:base + 2 * LANES]
            for c in range(N_HEADS):
                sl = slice(c * HEAD_W, (c + 1) * HEAD_W)
                r = _norm_rope(z[:, sl], aux[:, sl], cos, sin, d)
                if f32_ref is not None:
                    f32_ref[rows, c, :] = r
                zb_ref[rows, sl] = (r * q_scale).astype(zb_ref.dtype)

    def put_values(f32_ref):
        for r, rows, z in row_blocks():
            for c in range(N_HEADS):
                zc = z[:, c * HEAD_W:(c + 1) * HEAD_W]
                f32_ref[rows, c, :] = zc
                if emit_vt:
                    for u in range(sub // ATT_TILE):
                        chunk = r * (sub // ATT_TILE) + u
                        vt_ref[c, chunk, :HEAD_W, :] = (
                            zc[u * ATT_TILE:(u + 1) * ATT_TILE, :].T.astype(BF16))
                        vt_ref[c, chunk, HEAD_W:, :] = jnp.ones((VT_ONES, ATT_TILE), BF16)
            zb_ref[rows, :] = z.astype(zb_ref.dtype)

    def put_silu():
        for _, rows, z in row_blocks():
            zb_ref[rows, :] = (z * jax.nn.sigmoid(z)).astype(zb_ref.dtype)

    def put_sigmoid():
        for _, rows, z in row_blocks():
            zb_ref[rows, :] = jax.nn.sigmoid(z + aux).astype(zb_ref.dtype)

    pl.when(n == T_QA)(lambda: put_norm_rope(64, D_A ** -0.5 * LOG2E, None))
    pl.when(n == T_KA)(lambda: put_norm_rope(64, 1.0, kvd_ref))
    pl.when(n == T_VA)(lambda: put_values(kvd_ref))
    pl.when(n == T_QB)(lambda: put_norm_rope(128, 1.0, None))
    pl.when(n == T_KB)(lambda: put_norm_rope(128, 1.0, kvm_ref))
    pl.when(n == T_VB)(lambda: put_values(kvm_ref))
    pl.when((n == T_GA) | (n == T_GB))(put_silu)
    pl.when(n >= T_GM)(put_sigmoid)


def _proj(x, g, w_bf, tab, aux, kv_bufs, *, layer, depth, tm, z_dtype, vt_seq=None):
    rows, d_model = x.shape
    tab_blocks = tab.shape[0] // tm
    emit_vt = vt_seq is not None
    aliased = kv_bufs is not None
    clip01 = lambda v: jnp.minimum(jnp.maximum(v, 0), 1)
    in_specs = [
        pl.BlockSpec((tm, d_model), lambda m, n: (m, 0)),
        pl.BlockSpec((1, d_model), lambda m, n: (0, 0)),
        pl.BlockSpec((None, d_model, BRANCH_W), lambda m, n: (layer, 0, n)),
        pl.BlockSpec((tm, 4 * LANES), lambda m, n: (m % tab_blocks, 0)),
        pl.BlockSpec((1, 1, BRANCH_W), lambda m, n: (n, 0, 0)),
    ]
    args = [x, g, w_bf, tab, aux]
    if aliased:
        in_specs += [pl.BlockSpec(memory_space=pl.ANY)] * 2
        args += list(kv_bufs)
    kv_sds = jax.ShapeDtypeStruct((depth, rows, KV_ROWS, HEAD_W), F32)
    out_specs = [
        pl.BlockSpec((tm, BRANCH_W), lambda m, n: (m, n)),
        pl.BlockSpec((None, tm, N_HEADS, HEAD_W), lambda m, n: (layer, m, clip01(n - T_KA), 0)),
        pl.BlockSpec((None, tm, N_HEADS, HEAD_W), lambda m, n: (layer, m, clip01(n - T_KB), 0)),
    ]
    out_shape = [jax.ShapeDtypeStruct((rows, N_TILES * BRANCH_W), z_dtype), kv_sds, kv_sds]
    if emit_vt:
        assert vt_seq % tm == 0 and tm % ATT_TILE == 0
        mpb = vt_seq // tm
        out_specs.append(pl.BlockSpec(
            (None, N_HEADS, tm // ATT_TILE, VT_ROWS, ATT_TILE),
            lambda m, n: (m // mpb, jnp.where(n >= T_VB, 1, 0), m % mpb, 0, 0)))
        out_shape.append(jax.ShapeDtypeStruct(
            (rows // vt_seq, 2 * N_HEADS, vt_seq // ATT_TILE, VT_ROWS, ATT_TILE), BF16))
    return pl.pallas_call(
        functools.partial(_proj_kernel, emit_vt=emit_vt, aliased=aliased),
        grid=(rows // tm, N_TILES),
        in_specs=in_specs,
        out_specs=out_specs,
        out_shape=out_shape,
        input_output_aliases={5: 1, 6: 2} if aliased else {},
        scratch_shapes=[pltpu.VMEM((tm, d_model), BF16)],
        compiler_params=pltpu.CompilerParams(
            dimension_semantics=("parallel", "arbitrary"), vmem_limit_bytes=VMEM_LIMIT),
        name="proj",
    )(*args)


def _lambda_full(lp_ref, lam_init):
    a = jnp.sum(lp_ref[0:1, :] * lp_ref[1:2, :], axis=-1, keepdims=True)
    b = jnp.sum(lp_ref[2:3, :] * lp_ref[3:4, :], axis=-1, keepdims=True)
    return jnp.exp(a) - jnp.exp(b) + lam_init


def _subln(o, gain, lam_init):
    ms = jnp.mean(o * o, axis=-1, keepdims=True)
    return o * lax.rsqrt(ms + NORM_EPS) * gain * (1.0 - lam_init)


def _flash_heads(qi, score_diag, score_past, vt_ref, s_scr, p_scr, acc_scr):
    heads = list(range(s_scr.shape[1]))
    width = s_scr.shape[-1]
    per_group = max(1, MXU_GROUP_COLS // width)
    groups = [heads[i:i + per_group] for i in range(0, len(heads), per_group)]
    ones_row = jnp.ones((1, width), F32)

    def chunk_at(pos):
        return jnp.where(pos == 0, qi, pos - 1)

    def p_times_v(g, pos, slot):
        return jnp.dot(vt_ref[g, chunk_at(pos)], p_scr[slot, g], preferred_element_type=F32)

    def softmax_stage(slot, g, m, cm, keep):
        m_new = jnp.maximum(m, cm)
        a = jnp.exp2(m - m_new)
        m_sub = jnp.where(keep > 0.5, m_new, jnp.inf)
        for c in range(width // LANES):
            sl = slice(c * LANES, (c + 1) * LANES)
            p_scr[slot, g, :, sl] = jnp.exp2(s_scr[slot, g, :, sl] - m_sub[:, sl]).astype(BF16)
        return m_new, a

    def qk_stage(slot, score_next, group):
        scored = [score_next(g) for g in group]
        cms, keeps = [], []
        for g, (s, keep) in zip(group, scored):
            s_scr[slot, g] = s
            cm = jnp.max(s, axis=0, keepdims=True)
            keep = ones_row if keep is None else jnp.where(keep, 1.0, 0.0)
            cms.append(jnp.where(keep > 0.5, cm, -jnp.inf))
            keeps.append(keep)
        return tuple(cms), tuple(keeps)

    def step(pos, slot, carry, score_next):
        ms, cms, keeps = carry
        other = 1 - slot
        new_m, next_cm, next_keep = [], [], []
        for group in groups:
            if score_next is not None:
                cm2, keep2 = qk_stage(other, score_next, group)
            else:
                cm2, keep2 = [cms[g] for g in group], [keeps[g] for g in group]
            pvs = [p_times_v(g, jnp.maximum(pos - 1, 0), other) for g in group]
            for g, pv in zip(group, pvs):
                m_new, a = softmax_stage(slot, g, ms[g], cms[g], keeps[g])
                acc_scr[g] = a * (acc_scr[g] + pv)
                new_m.append(m_new)
            next_cm += list(cm2)
            next_keep += list(keep2)
        return tuple(new_m), tuple(next_cm), tuple(next_keep)

    for g in heads:
        p_scr[1, g] = jnp.zeros(p_scr.shape[2:], BF16)
        acc_scr[g] = jnp.zeros(acc_scr.shape[1:], F32)
    cm0, keep0 = [], []
    for group in groups:
        cm2, keep2 = qk_stage(0, lambda g: (score_diag(g), None), group)
        cm0 += list(cm2)
        keep0 += list(keep2)
    carry = (tuple(jnp.full((1, width), -jnp.inf, F32) for _ in heads), tuple(cm0), tuple(keep0))

    def pair(u, carry):
        pos = 2 * u
        carry = step(pos, 0, carry, lambda g: score_past(g, pos))
        return step(pos + 1, 1, carry, lambda g: score_past(g, pos + 1))

    carry = lax.fori_loop(0, qi // 2, pair, carry)
    carry = lax.cond(qi % 2 == 1,
                     lambda c: step(qi - 1, 0, c, lambda g: score_past(g, qi - 1)),
                     lambda c: c, carry)
    step(qi, qi % 2, carry, None)
    return [acc_scr[g] + p_times_v(g, qi, qi % 2) for g in heads]


def _top_blocks(gate, n_valid, axis):
    idx = lax.broadcasted_iota(jnp.int32, gate.shape, axis)
    size = gate.shape[axis]
    g = jnp.where(idx < n_valid, gate, -jnp.inf)
    sel = jnp.zeros(gate.shape, F32)
    for r in range(MOBA_TOPK):
        mx = jnp.max(g, axis=axis, keepdims=True)
        pick = jnp.min(jnp.where(g == mx, idx, size), axis=axis, keepdims=True)
        hit = idx == pick
        sel = jnp.maximum(sel, jnp.where(hit, jnp.where(r < n_valid, 1.0, 0.0), 0.0))
        g = jnp.where(hit, -jnp.inf, g)
    return sel


def _head_cols(g):
    return slice(g * HEAD_W, (g + 1) * HEAD_W)


def _diff_attn_kernel(q_ref, k_ref, vt_ref, lp_ref, sg_ref, o_ref, s_scr, p_scr, acc_scr, *,
                      lam_init):
    qi = pl.program_id(2)
    tq = ATT_TILE
    key = lax.broadcasted_iota(jnp.int32, (tq, 2 * tq), 0)
    qcol = lax.broadcasted_iota(jnp.int32, (tq, 2 * tq), 1) % tq
    lane = lax.broadcasted_iota(jnp.int32, (tq, HEAD_W), 1)

    q12 = []
    for g in range(s_scr.shape[1]):
        q = q_ref[:, _head_cols(g)]
        zero = jnp.zeros_like(q)
        q12.append(jnp.concatenate([jnp.where(lane < D_A, q, zero),
                                    jnp.where(lane < D_A, zero, q)], axis=0))

    def scores(g, j):
        off = pl.multiple_of(j * tq, tq)
        return lax.dot_general(k_ref[pl.ds(off, tq), _head_cols(g)], q12[g], _NT,
                               preferred_element_type=F32)

    outs = _flash_heads(qi, lambda g: jnp.where(key <= qcol, scores(g, qi), -jnp.inf),
                        lambda g, j: (scores(g, j), None), vt_ref, s_scr, p_scr, acc_scr)
    lam = _lambda_full(lp_ref, lam_init)
    for g, acc in enumerate(outs):
        o = acc[:HEAD_W, :] / acc[HEAD_W:HEAD_W + 1, :]
        d = (o[:, :tq] - lam * o[:, tq:]).T
        o_ref[:, _head_cols(g)] = _subln(d, sg_ref[...], lam_init)


def _attn_specs(branch, seq, nq, nh):
    q_tile, k_tile = (T_QA, T_KA) if branch == 0 else (T_QB, T_KB)
    groups = N_HEADS // nh
    width = nh * HEAD_W
    return [
        pl.BlockSpec((ATT_TILE, width), lambda b, h, i: (b * nq + i, q_tile * groups + h)),
        pl.BlockSpec((seq, width), lambda b, h, i: (b, k_tile * groups + h)),
        pl.BlockSpec((None, nh, nq, VT_ROWS, ATT_TILE),
                     lambda b, h, i: (b, branch * groups + h, 0, 0, 0)),
    ]


def _flash_scratch(width, nh):
    return [pltpu.VMEM((2, nh, ATT_TILE, width), F32),
            pltpu.VMEM((2, nh, ATT_TILE, width), BF16),
            pltpu.VMEM((nh, VT_ROWS, width), F32)]


def _attn_call(kern, name, branch, nh, extra_specs, scratch, args, *, batch, seq):
    nq = seq // ATT_TILE
    width = nh * HEAD_W
    return pl.pallas_call(
        kern,
        grid=(batch, N_HEADS // nh, nq),
        in_specs=_attn_specs(branch, seq, nq, nh) + extra_specs,
        out_specs=pl.BlockSpec((ATT_TILE, width), lambda b, h, i: (b * nq + i, h)),
        out_shape=jax.ShapeDtypeStruct((batch * seq, BRANCH_W), F32),
        scratch_shapes=scratch,
        compiler_params=pltpu.CompilerParams(
            dimension_semantics=("parallel", "parallel", "arbitrary"),
            vmem_limit_bytes=VMEM_LIMIT),
        name=name,
    )(*args)


def _diff_attn(zb, vt, lam_params, subln_g, *, batch, seq, lam_init):
    kern = functools.partial(_diff_attn_kernel, lam_init=lam_init)
    extra = [pl.BlockSpec((4, D_A), lambda b, h, i: (0, 0)),
             pl.BlockSpec((1, HEAD_W), lambda b, h, i: (0, 0))]
    return _attn_call(kern, "diff_attn", 0, DIFF_HEADS, extra,
                      _flash_scratch(2 * ATT_TILE, DIFF_HEADS),
                      (zb, zb, vt, lam_params, subln_g), batch=batch, seq=seq)


def _moba_attn_kernel(q_ref, k_ref, vt_ref, o_ref, kmean_scr, sel_scr, s_scr, p_scr, acc_scr, *,
                      nb):
    qi = pl.program_id(2)
    tq = ATT_TILE
    scale = HEAD_W ** -0.5 * LOG2E
    heads = range(s_scr.shape[1])

    @pl.when(qi == 0)
    def _():
        kmean_scr[...] = jnp.zeros(kmean_scr.shape, F32)
        for g in heads:
            for j in range(nb):
                blk = k_ref[j * tq:(j + 1) * tq, _head_cols(g)].astype(F32)
                kmean_scr[g, j:j + 1, :] = jnp.mean(blk, axis=0, keepdims=True)

    key = lax.broadcasted_iota(jnp.int32, (tq, tq), 0)
    qcol = lax.broadcasted_iota(jnp.int32, (tq, tq), 1)
    qs = [q_ref[:, _head_cols(g)] for g in heads]

    def scores(g, j):
        off = pl.multiple_of(j * tq, tq)
        return lax.dot_general(k_ref[pl.ds(off, tq), _head_cols(g)], qs[g], _NT,
                               preferred_element_type=F32) * scale

    for g in heads:
        gate = lax.dot_general(kmean_scr[g], qs[g].astype(F32), _NT,
                               precision=lax.Precision.HIGHEST, preferred_element_type=F32)
        sel_scr[g] = _top_blocks(gate, qi, axis=0)

    def score_past(g, j):
        return scores(g, j), sel_scr[g, pl.ds(j, 1), :] > 0.5

    outs = _flash_heads(qi, lambda g: jnp.where(key <= qcol, scores(g, qi), -jnp.inf), score_past,
                        vt_ref, s_scr, p_scr, acc_scr)
    for g, acc in enumerate(outs):
        o_ref[:, _head_cols(g)] = (acc[:HEAD_W, :] / acc[HEAD_W:HEAD_W + 1, :]).T


def _moba_attn(zb, vt, *, batch, seq):
    assert ATT_TILE == MOBA_BLOCK
    nq = seq // ATT_TILE
    nb_pad = -(-nq // 8) * 8
    kern = functools.partial(_moba_attn_kernel, nb=nq)
    scratch = [pltpu.VMEM((MOBA_HEADS, nb_pad, HEAD_W), F32),
               pltpu.VMEM((MOBA_HEADS, nb_pad, ATT_TILE), F32)] + _flash_scratch(ATT_TILE, MOBA_HEADS)
    return _attn_call(kern, "moba_attn", 1, MOBA_HEADS, [], scratch, (zb, zb, vt),
                      batch=batch, seq=seq)


Q_ROWS = 16


def _state_rows(h):
    return slice(Q_ROWS * h, Q_ROWS * (h + 1))


def _cached_rows(kv_refs, gs, kv):
    return jnp.concatenate(
        [kv_refs[g][:, kv].reshape(-1, HEAD_W).astype(BF16) for g in gs], axis=0)


def _own_head_mask():
    row_head = lax.broadcasted_iota(jnp.int32, (N_HEADS * Q_ROWS, LANES), 0) // Q_ROWS
    col_head = lax.broadcasted_iota(jnp.int32, (N_HEADS * Q_ROWS, LANES), 1) % N_HEADS
    return row_head == col_head


def _mask_cols(y, mask, fill):
    return jnp.concatenate(
        [jnp.where(mask, y[:, t * LANES:(t + 1) * LANES], fill) for t in range(y.shape[1] // LANES)],
        axis=1)


def _own_p_times_v(pb, vn_ref):
    return jnp.concatenate(
        [jnp.dot(pb[_state_rows(h), :], _pad_rows(vn_ref[:, _head_cols(h)]).astype(BF16),
                 preferred_element_type=F32) for h in range(N_HEADS)], axis=0)


def _pad_rows(x_f32):
    return jnp.concatenate([x_f32, jnp.zeros((LANES - x_f32.shape[0], x_f32.shape[1]), F32)], axis=0)


def _head_queries(q_ref, h, halves, q_scale=1.0):
    q = q_ref[:, HEAD_W * h:HEAD_W * (h + 1)]
    zero = jnp.zeros_like(q)
    if halves:
        lane = lax.broadcasted_iota(jnp.int32, q.shape, 1)
        q16 = jnp.concatenate([jnp.where(lane < D_A, q, zero), jnp.where(lane < D_A, zero, q)], axis=0)
    else:
        q16 = jnp.concatenate([q * q_scale, zero], axis=0)
    return q16.astype(BF16)


def _own_scores(q16, kn_ref, h, dec):
    kn = _pad_rows(kn_ref[:, HEAD_W * h:HEAD_W * (h + 1)]).astype(BF16)
    s = lax.dot_general(q16, kn, _NT, preferred_element_type=F32)
    key = lax.broadcasted_iota(jnp.int32, s.shape, 1)
    t = lax.broadcasted_iota(jnp.int32, s.shape, 0) % dec
    return s, key <= t


def _diff_dec_kernel(pt_ref, q_ref, kn_ref, vn_ref, lp_ref, sg_ref, *rest, pages, lam_init):
    kv_refs = rest[:pages]
    o_ref = rest[pages]
    q_scr, m_scr, l_scr, acc_scr = rest[pages + 1:]
    c = pl.program_id(1)
    dec = q_ref.shape[0]

    @pl.when(c == 0)
    def _():
        for h in range(N_HEADS):
            q_scr[_state_rows(h), :] = _head_queries(q_ref, h, halves=True)
        m_scr[...] = jnp.full(m_scr.shape, -jnp.inf, F32)
        l_scr[...] = jnp.zeros(l_scr.shape, F32)
        acc_scr[...] = jnp.zeros(acc_scr.shape, F32)

    def update(s, p_times_v):
        m_old = m_scr[...]
        m_new = jnp.maximum(m_old, jnp.max(s, axis=1, keepdims=True))
        alpha = jnp.exp2(m_old - m_new)
        p = jnp.exp2(s - m_new[:, :1])
        l_scr[...] = alpha * l_scr[...] + jnp.sum(p, axis=1, keepdims=True)
        m_scr[...] = m_new
        acc_scr[...] = alpha * acc_scr[...] + p_times_v(p.astype(BF16))

    halves = [range(0, pages // 2), range(pages // 2, pages)]
    own_head = _own_head_mask()
    ys = [lax.dot_general(q_scr[...], _cached_rows(kv_refs, gs, 0), _NT,
                          preferred_element_type=F32) for gs in halves]
    for gs, y in zip(halves, ys):
        v_all = _cached_rows(kv_refs, gs, 1)
        update(_mask_cols(y, own_head, -jnp.inf),
               lambda pb: jnp.dot(pb, v_all, preferred_element_type=F32))

    @pl.when(c == pl.num_programs(1) - 1)
    def _():
        owns = [_own_scores(q_scr[_state_rows(h), :], kn_ref, h, dec) for h in range(N_HEADS)]
        s = jnp.concatenate([jnp.where(ok, s_h, -jnp.inf) for s_h, ok in owns], axis=0)
        update(s, lambda pb: _own_p_times_v(pb, vn_ref))
        o = acc_scr[...] / l_scr[...]
        lam = _lambda_full(lp_ref, lam_init)
        for h in range(N_HEADS):
            r = Q_ROWS * h
            d = o[r:r + dec, :] - lam * o[r + 8:r + 8 + dec, :]
            o_ref[:, _head_cols(h)] = _subln(d, sg_ref[...], lam_init)


def _dec_specs(dec, page, pages, layer, tiles):
    row_specs = [pl.BlockSpec((dec, BRANCH_W), lambda b, c, pt, tile=tile: (b, tile))
                 for tile in tiles]
    cache_specs = [
        pl.BlockSpec((None, None, page, 2, N_HEADS, HEAD_W),
                     lambda b, c, pt, g=g: (layer, pt[b, c * pages + g], 0, 0, 0, 0))
        for g in range(pages)]
    return row_specs, cache_specs


def _diff_dec(page_table, zs, cache, lam_params, subln_g, *, layer, dec, pages, lam_init):
    n_seq, n_pages = page_table.shape
    page = cache.shape[2]
    assert dec == 8 and n_pages % pages == 0 and pages % 2 == 0
    kern = functools.partial(_diff_dec_kernel, pages=pages, lam_init=lam_init)
    row_specs, cache_specs = _dec_specs(dec, page, pages, layer, (T_QA, T_KA, T_VA))
    state = pltpu.VMEM((N_HEADS * Q_ROWS, LANES), F32)
    return pl.pallas_call(
        kern,
        grid_spec=pltpu.PrefetchScalarGridSpec(
            num_scalar_prefetch=1,
            grid=(n_seq, n_pages // pages),
            in_specs=row_specs + [pl.BlockSpec((4, D_A), lambda b, c, pt: (0, 0)),
                                  pl.BlockSpec((1, HEAD_W), lambda b, c, pt: (0, 0))] + cache_specs,
            out_specs=pl.BlockSpec((dec, BRANCH_W), lambda b, c, pt: (b, 0)),
            scratch_shapes=[pltpu.VMEM((N_HEADS * Q_ROWS, HEAD_W), BF16), state, state, state]),
        out_shape=jax.ShapeDtypeStruct((n_seq * dec, BRANCH_W), F32),
        compiler_params=pltpu.CompilerParams(
            dimension_semantics=("parallel", "arbitrary"), vmem_limit_bytes=VMEM_LIMIT),
        name="diff_dec",
    )(page_table, zs, zs, zs, lam_params, subln_g, *([cache] * pages))


def _moba_dec_kernel(pt_ref, q_ref, kn_ref, vn_ref, *rest, pages, page, n_blocks):
    kv_refs = rest[:pages]
    o_ref = rest[pages]
    q_scr, gate_scr, m_scr, l_scr, acc_scr = rest[pages + 1:]
    c = pl.program_id(1)
    dec = q_ref.shape[0]
    scale = HEAD_W ** -0.5 * LOG2E
    pages_per_block = MOBA_BLOCK // page
    blocks_per_step = pages // pages_per_block
    blk_lane = lax.broadcasted_iota(jnp.int32, (N_HEADS * Q_ROWS, LANES), 1)

    @pl.when(c == 0)
    def _():
        for h in range(N_HEADS):
            q_scr[_state_rows(h), :] = _head_queries(q_ref, h, halves=False, q_scale=scale)
        gate_scr[...] = jnp.zeros(gate_scr.shape, F32)
        m_scr[...] = jnp.zeros(m_scr.shape, F32)
        l_scr[...] = jnp.zeros(l_scr.shape, F32)

    def block_softmax(s, p_times_v):
        m = jnp.max(s, axis=1, keepdims=True)
        p = jnp.exp2(s - m)
        return m, jnp.sum(p, axis=1, keepdims=True), p_times_v(p.astype(BF16))

    def block_pages(blk):
        return range(blk * pages_per_block, (blk + 1) * pages_per_block)

    own_head = _own_head_mask()
    ys = [lax.dot_general(q_scr[...], _cached_rows(kv_refs, block_pages(blk), 0), _NT,
                          preferred_element_type=F32) for blk in range(blocks_per_step)]
    for blk in range(blocks_per_step):
        j = c * blocks_per_step + blk
        here = blk_lane == j
        gate = jnp.sum(_mask_cols(ys[blk], own_head, 0.0), axis=1, keepdims=True) * (
            1.0 / (MOBA_BLOCK * scale))
        v_all = _cached_rows(kv_refs, block_pages(blk), 1)
        m, l, acc = block_softmax(_mask_cols(ys[blk], own_head, -jnp.inf),
                                  lambda pb: jnp.dot(pb, v_all, preferred_element_type=F32))
        gate_scr[...] = jnp.where(here, gate, gate_scr[...])
        m_scr[...] = jnp.where(here, m, m_scr[...])
        l_scr[...] = jnp.where(here, l, l_scr[...])
        acc_scr[j] = acc

    @pl.when(c == pl.num_programs(1) - 1)
    def _():
        owns = [_own_scores(q_scr[_state_rows(h), :], kn_ref, h, dec) for h in range(N_HEADS)]
        s = jnp.concatenate([jnp.where(ok, s_h, -jnp.inf) for s_h, ok in owns], axis=0)
        m_own, l_own, acc_own = block_softmax(s, lambda pb: _own_p_times_v(pb, vn_ref))
        picked = _top_blocks(gate_scr[...], n_blocks, axis=1) > 0.5
        m_all = m_scr[...]
        m_fin = jnp.maximum(m_own, jnp.max(jnp.where(picked, m_all, -jnp.inf),
                                           axis=1, keepdims=True))
        w = jnp.where(picked, jnp.exp2(m_all - m_fin), 0.0)
        w_own = jnp.exp2(m_own - m_fin)
        l_fin = w_own * l_own + jnp.sum(w * l_scr[...], axis=1, keepdims=True)
        acc = w_own * acc_own
        for jb in range(n_blocks):
            acc = acc + w[:, jb:jb + 1] * acc_scr[jb]
        o = acc / l_fin
        for h in range(N_HEADS):
            o_ref[:, _head_cols(h)] = o[Q_ROWS * h:Q_ROWS * h + dec, :]


def _moba_dec(page_table, zs, cache, *, layer, dec, pages):
    n_seq, n_pages = page_table.shape
    page = cache.shape[2]
    past = n_pages * page
    assert dec == 8 and past % MOBA_BLOCK == 0 and MOBA_BLOCK % page == 0
    assert pages % (MOBA_BLOCK // page) == 0 and n_pages % pages == 0
    n_blocks = past // MOBA_BLOCK
    assert MOBA_TOPK <= n_blocks <= LANES
    kern = functools.partial(_moba_dec_kernel, pages=pages, page=page, n_blocks=n_blocks)
    row_specs, cache_specs = _dec_specs(dec, page, pages, layer, (T_QB, T_KB, T_VB))
    stat = pltpu.VMEM((N_HEADS * Q_ROWS, LANES), F32)
    return pl.pallas_call(
        kern,
        grid_spec=pltpu.PrefetchScalarGridSpec(
            num_scalar_prefetch=1,
            grid=(n_seq, n_pages // pages),
            in_specs=row_specs + cache_specs,
            out_specs=pl.BlockSpec((dec, BRANCH_W), lambda b, c, pt: (b, 0)),
            scratch_shapes=[pltpu.VMEM((N_HEADS * Q_ROWS, HEAD_W), BF16), stat, stat, stat,
                            pltpu.VMEM((n_blocks, N_HEADS * Q_ROWS, HEAD_W), F32)]),
        out_shape=jax.ShapeDtypeStruct((n_seq * dec, BRANCH_W), F32),
        compiler_params=pltpu.CompilerParams(
            dimension_semantics=("parallel", "arbitrary"), vmem_limit_bytes=VMEM_LIMIT),
        name="moba_dec",
    )(page_table, zs, zs, zs, *([cache] * pages))


def _out_kernel(x_ref, oa_ref, ob_ref, ga_ref, gb_ref, gm_ref, wa_ref, wb_ref, wo_ref, y_ref):
    d_model = x_ref.shape[1]
    ua = (oa_ref[...] * ga_ref[...].astype(F32)).astype(BF16)
    ub = (ob_ref[...] * gb_ref[...].astype(F32)).astype(BF16)
    ya = jnp.dot(ua, wa_ref[...], preferred_element_type=F32)
    yb = jnp.dot(ub, wb_ref[...], preferred_element_type=F32)
    merged = (gm_ref[:, :d_model].astype(F32) * ya + gm_ref[:, d_model:].astype(F32) * yb)
    y_ref[...] = x_ref[...] + jnp.dot(merged.astype(BF16), wo_ref[...],
                                      preferred_element_type=F32)


def _out(x, oa, ob, zb, wa, wb, wo, *, layer, tm):
    rows, d_model = x.shape
    assert 2 * d_model == 4 * BRANCH_W
    const = lambda shape: pl.BlockSpec((None,) + shape, lambda m: (layer, 0, 0),
                                       pipeline_mode=pl.Buffered(1))
    return pl.pallas_call(
        _out_kernel,
        grid=(rows // tm,),
        in_specs=[
            pl.BlockSpec((tm, d_model), lambda m: (m, 0)),
            pl.BlockSpec((tm, BRANCH_W), lambda m: (m, 0)),
            pl.BlockSpec((tm, BRANCH_W), lambda m: (m, 0)),
            pl.BlockSpec((tm, BRANCH_W), lambda m: (m, T_GA)),
            pl.BlockSpec((tm, BRANCH_W), lambda m: (m, T_GB)),
            pl.BlockSpec((tm, 2 * d_model), lambda m: (m, T_GM * BRANCH_W // (2 * d_model))),
            const((BRANCH_W, d_model)),
            const((BRANCH_W, d_model)),
            const((d_model, d_model)),
        ],
        out_specs=pl.BlockSpec((tm, d_model), lambda m: (m, 0)),
        out_shape=jax.ShapeDtypeStruct((rows, d_model), F32),
        compiler_params=pltpu.CompilerParams(
            dimension_semantics=("parallel",), vmem_limit_bytes=VMEM_LIMIT),
        name="out",
    )(x, oa, ob, zb, zb, zb, wa, wb, wo)


def _rope_table(pos):
    pos = pos.astype(F32)[:, None]
    parts = []
    for d in (D_A, HEAD_W):
        half = d // 2
        inv = ROPE_THETA ** (-2.0 * jnp.arange(half, dtype=F32) / d)
        ang = pos * inv[None, :]
        cos, sin = jnp.cos(ang), jnp.sin(ang)
        reps = LANES // d
        parts.append(jnp.tile(jnp.concatenate([cos, cos], axis=-1), (1, reps)))
        parts.append(jnp.tile(jnp.concatenate([-sin, sin], axis=-1), (1, reps)))
    return jnp.concatenate(parts, axis=-1)


def _aux_rows(q_norm_a, k_norm_a, q_norm_b, k_norm_b, b_merge):
    zero = jnp.zeros((BRANCH_W,), F32)
    rows = [jnp.tile(q_norm_a, BRANCH_W // D_A), jnp.tile(k_norm_a, BRANCH_W // D_A), zero, zero,
            jnp.tile(q_norm_b, BRANCH_W // HEAD_W), jnp.tile(k_norm_b, BRANCH_W // HEAD_W),
            zero, zero]
    rows += list(b_merge.reshape(-1, BRANCH_W))
    return jnp.stack(rows)[:, None, :]


def kernel(x_prompt, x_sample, cache_kv_diff, cache_kv_moba, page_table, norm_gain, w_in, q_norm_a, k_norm_a, lambda_q1, lambda_k1, lambda_q2, lambda_k2, subln_gain, q_norm_b, k_norm_b, b_merge, w_up_a, w_up_b, w_out):
    batch, seq, d_model = x_prompt.shape
    n_seq, dec, _ = x_sample.shape
    depth, n_pool, page = cache_kv_diff.shape[:3]
    n_pages = page_table.shape[1]
    past = n_pages * page
    assert w_in.shape[2] == N_TILES * BRANCH_W and seq % MOBA_BLOCK == 0

    cache_d, cache_m = cache_kv_diff, cache_kv_moba
    assert cache_d.shape[3:] == cache_m.shape[3:] == (2, N_HEADS, HEAD_W)
    tab_p = _rope_table(jnp.arange(seq))
    tab_s = jnp.tile(_rope_table(past + jnp.arange(dec)), (n_seq, 1))

    yp = x_prompt.reshape(batch * seq, d_model)
    ys = x_sample.reshape(n_seq * dec, d_model)
    kv_p = kv_s = None
    w_in_l = w_in.astype(BF16)
    wa, wb, wo = w_up_a.astype(BF16), w_up_b.astype(BF16), w_out.astype(BF16)
    for l in range(depth):
        lam_init = _lambda_init(l)
        g = norm_gain[l][None, :]
        aux = _aux_rows(q_norm_a[l], k_norm_a[l], q_norm_b[l], k_norm_b[l], b_merge[l])
        lam_params = jnp.stack([lambda_q1[l], lambda_k1[l], lambda_q2[l], lambda_k2[l]])
        sg = subln_gain[l][None, :]

        zb, *kv_p, vt = _proj(yp, g, w_in_l, tab_p, aux, kv_p, layer=l, depth=depth, tm=512,
                              z_dtype=BF16, vt_seq=seq)
        oa = _diff_attn(zb, vt, lam_params, sg, batch=batch, seq=seq, lam_init=lam_init)
        ob = _moba_attn(zb, vt, batch=batch, seq=seq)
        yp = _out(yp, oa, ob, zb, wa, wb, wo, layer=l, tm=256)

        zs, *kv_s = _proj(ys, g, w_in_l, tab_s, aux, kv_s, layer=l, depth=depth, tm=n_seq * dec,
                          z_dtype=F32)
        oa = _diff_dec(page_table, zs, cache_d, lam_params, sg, layer=l, dec=dec, pages=8,
                       lam_init=lam_init)
        ob = _moba_dec(page_table, zs, cache_m, layer=l, dec=dec, pages=8)
        ys = _out(ys, oa, ob, zs, wa, wb, wo, layer=l, tm=n_seq * dec)

    kv_shape = lambda rows_a, rows_b: (depth, rows_a, rows_b, 2, N_HEADS, HEAD_W)
    return (yp.reshape(batch, seq, d_model),
            ys.reshape(n_seq, dec, d_model),
            kv_p[0].reshape(kv_shape(batch, seq)),
            kv_p[1].reshape(kv_shape(batch, seq)),
            kv_s[0].reshape(kv_shape(n_seq, dec)),
            kv_s[1].reshape(kv_shape(n_seq, dec)))
```

```python
import functools
import math

import jax
import jax.numpy as jnp
from jax import lax
from jax.experimental import pallas as pl
from jax.experimental.pallas import tpu as pltpu

F32 = jnp.float32
BF16 = jnp.bfloat16

ROPE_THETA = 10000.0
NORM_EPS = 1e-6
N_HEADS = 8
HEAD_W = 128
BRANCH_W = N_HEADS * HEAD_W
D_A = 64
MOBA_BLOCK = 256
MOBA_TOPK = 3
LANES = 128
ATT_TILE = 256
PROJ_SUB = 256
LOG2E = math.log2(math.e)
VT_ONES = 16
VT_ROWS = HEAD_W + VT_ONES
DIFF_HEADS = 4
MOBA_HEADS = 4
MXU_GROUP_COLS = 1024
DEC_SPAN = 8
DEC_PAGES = 16
KV_ROWS = 2 * N_HEADS
VMEM_LIMIT = 56 * 1024 * 1024

T_QA, T_KA, T_VA, T_GA, T_QB, T_KB, T_VB, T_GB, T_GM = 0, 1, 2, 3, 4, 5, 6, 7, 8
N_TILES = 12

_NT = (((1,), (1,)), ((), ()))


def _lambda_init(layer):
    return 0.8 - 0.6 * math.exp(-0.3 * layer)


def _norm_rope(x, gain, cos, sin, d):
    lane = lax.broadcasted_iota(jnp.int32, x.shape, 1)
    x2 = x * x
    if d == 64:
        lo = lane < 64
        s_lo = jnp.sum(jnp.where(lo, x2, 0.0), axis=-1, keepdims=True)
        s_hi = jnp.sum(jnp.where(lo, 0.0, x2), axis=-1, keepdims=True)
        ms = jnp.where(lo, s_lo, s_hi) * (1.0 / 64.0)
    else:
        ms = jnp.sum(x2, axis=-1, keepdims=True) * (1.0 / 128.0)
    y = x * lax.rsqrt(ms + NORM_EPS) * gain
    if d == 64:
        partner = jnp.where((lane & 63) < 32, pltpu.roll(y, 96, 1), pltpu.roll(y, 32, 1))
    else:
        partner = pltpu.roll(y, 64, 1)
    return y * cos + partner * sin


def _proj_kernel(*refs, emit_vt, aliased):
    x_ref, g_ref, w_ref, tab_ref, aux_ref = refs[:5]
    outs = refs[5 + (2 if aliased else 0):]
    zb_ref, kvd_ref, kvm_ref = outs[:3]
    vt_ref = outs[3] if emit_vt else None
    h_scr = outs[-1]
    n = pl.program_id(1)

    @pl.when(n == 0)
    def _():
        x = x_ref[...]
        ms = jnp.mean(x * x, axis=-1, keepdims=True)
        h_scr[...] = (x * lax.rsqrt(ms + NORM_EPS) * g_ref[...]).astype(BF16)

    aux = aux_ref[0]
    tm = h_scr.shape[0]
    sub = min(tm, PROJ_SUB)

    def row_blocks():
        for r in range(tm // sub):
            rows = slice(r * sub, (r + 1) * sub)
            yield r, rows, jnp.dot(h_scr[rows, :], w_ref[...], preferred_element_type=F32)

    def put_norm_rope(d, q_scale, f32_ref):
        base = 0 if d == 64 else 2 * LANES
        for _, rows, z in row_blocks():
            cos = tab_ref[rows, base:base + LANES]
            sin = tab_ref[rows, base + LANES:base + 2 * LANES]
            for c in range(N_HEADS):
                sl = slice(c * HEAD_W, (c + 1) * HEAD_W)
                r = _norm_rope(z[:, sl], aux[:, sl], cos, sin, d)
                if f32_ref is not None:
                    f32_ref[rows, c, :] = r
                zb_ref[rows, sl] = (r * q_scale).astype(zb_ref.dtype)

    def put_values(f32_ref):
        for r, rows, z in row_blocks():
            for c in range(N_HEADS):
                zc = z[:, c * HEAD_W:(c + 1) * HEAD_W]
                f32_ref[rows, c, :] = zc
                if emit_vt:
                    for u in range(sub // LANES):
                        chunk, col = divmod(r * sub + u * LANES, ATT_TILE)
                        vt_ref[c, chunk, :HEAD_W, col:col + LANES] = (
                            zc[u * LANES:(u + 1) * LANES, :].T.astype(BF16))
                        if col == 0:
                            vt_ref[c, chunk, HEAD_W:, :] = jnp.ones((VT_ONES, ATT_TILE), BF16)
            zb_ref[rows, :] = z.astype(zb_ref.dtype)

    def put_silu():
        for _, rows, z in row_blocks():
            zb_ref[rows, :] = (z * jax.nn.sigmoid(z)).astype(zb_ref.dtype)

    def put_sigmoid():
        for _, rows, z in row_blocks():
            zb_ref[rows, :] = jax.nn.sigmoid(z + aux).astype(zb_ref.dtype)

    pl.when(n == T_QA)(lambda: put_norm_rope(64, D_A ** -0.5 * LOG2E, None))
    pl.when(n == T_KA)(lambda: put_norm_rope(64, 1.0, kvd_ref))
    pl.when(n == T_VA)(lambda: put_values(kvd_ref))
    pl.when(n == T_QB)(lambda: put_norm_rope(128, 1.0, None))
    pl.when(n == T_KB)(lambda: put_norm_rope(128, 1.0, kvm_ref))
    pl.when(n == T_VB)(lambda: put_values(kvm_ref))
    pl.when((n == T_GA) | (n == T_GB))(put_silu)
    pl.when(n >= T_GM)(put_sigmoid)


def _proj(x, g, w_bf, tab, aux, kv_bufs, *, layer, depth, tm, z_dtype, vt_seq=None):
    rows, d_model = x.shape
    tab_blocks = tab.shape[0] // tm
    emit_vt = vt_seq is not None
    aliased = kv_bufs is not None
    clip01 = lambda v: jnp.minimum(jnp.maximum(v, 0), 1)
    in_specs = [
        pl.BlockSpec((tm, d_model), lambda m, n: (m, 0)),
        pl.BlockSpec((1, d_model), lambda m, n: (0, 0)),
        pl.BlockSpec((None, d_model, BRANCH_W), lambda m, n: (layer, 0, n)),
        pl.BlockSpec((tm, 4 * LANES), lambda m, n: (m % tab_blocks, 0)),
        pl.BlockSpec((1, 1, BRANCH_W), lambda m, n: (n, 0, 0)),
    ]
    args = [x, g, w_bf, tab, aux]
    if aliased:
        in_specs += [pl.BlockSpec(memory_space=pl.ANY)] * 2
        args += list(kv_bufs)
    kv_sds = jax.ShapeDtypeStruct((depth, rows, KV_ROWS, HEAD_W), F32)
    out_specs = [
        pl.BlockSpec((tm, BRANCH_W), lambda m, n: (m, n)),
        pl.BlockSpec((None, tm, N_HEADS, HEAD_W), lambda m, n: (layer, m, clip01(n - T_KA), 0)),
        pl.BlockSpec((None, tm, N_HEADS, HEAD_W), lambda m, n: (layer, m, clip01(n - T_KB), 0)),
    ]
    out_shape = [jax.ShapeDtypeStruct((rows, N_TILES * BRANCH_W), z_dtype), kv_sds, kv_sds]
    if emit_vt:
        assert vt_seq % tm == 0 and tm % ATT_TILE == 0
        mpb = vt_seq // tm
        out_specs.append(pl.BlockSpec(
            (None, N_HEADS, tm // ATT_TILE, VT_ROWS, ATT_TILE),
            lambda m, n: (m // mpb, jnp.where(n >= T_VB, 1, 0), m % mpb, 0, 0)))
        out_shape.append(jax.ShapeDtypeStruct(
            (rows // vt_seq, 2 * N_HEADS, vt_seq // ATT_TILE, VT_ROWS, ATT_TILE), BF16))
    return pl.pallas_call(
        functools.partial(_proj_kernel, emit_vt=emit_vt, aliased=aliased),
        grid=(rows // tm, N_TILES),
        in_specs=in_specs,
        out_specs=out_specs,
        out_shape=out_shape,
        input_output_aliases={5: 1, 6: 2} if aliased else {},
        scratch_shapes=[pltpu.VMEM((tm, d_model), BF16)],
        compiler_params=pltpu.CompilerParams(
            dimension_semantics=("parallel", "arbitrary"), vmem_limit_bytes=VMEM_LIMIT),
        name="proj",
    )(*args)


def _lambda_full(lp_ref, lam_init):
    a = jnp.sum(lp_ref[0:1, :] * lp_ref[1:2, :], axis=-1, keepdims=True)
    b = jnp.sum(lp_ref[2:3, :] * lp_ref[3:4, :], axis=-1, keepdims=True)
    return jnp.exp(a) - jnp.exp(b) + lam_init


def _subln(o, gain, lam_init):
    ms = jnp.mean(o * o, axis=-1, keepdims=True)
    return o * lax.rsqrt(ms + NORM_EPS) * gain * (1.0 - lam_init)


def _flash_heads(qi, score_diag, score_past, vt_ref, s_scr, p_scr, acc_scr):
    heads = list(range(s_scr.shape[1]))
    width = s_scr.shape[-1]
    per_group = max(1, MXU_GROUP_COLS // width)
    groups = [heads[i:i + per_group] for i in range(0, len(heads), per_group)]
    ones_row = jnp.ones((1, width), F32)

    def chunk_at(pos):
        return jnp.where(pos == 0, qi, pos - 1)

    def p_times_v(g, pos, slot):
        return jnp.dot(vt_ref[g, chunk_at(pos)], p_scr[slot, g], preferred_element_type=F32)

    def softmax_stage(slot, g, m, cm, keep):
        m_new = jnp.maximum(m, cm)
        a = jnp.exp2(m - m_new)
        m_sub = jnp.where(keep > 0.5, m_new, jnp.inf)
        for c in range(width // LANES):
            sl = slice(c * LANES, (c + 1) * LANES)
            p_scr[slot, g, :, sl] = jnp.exp2(s_scr[slot, g, :, sl] - m_sub[:, sl]).astype(BF16)
        return m_new, a

    def qk_stage(slot, score_next, group):
        scored = [score_next(g) for g in group]
        cms, keeps = [], []
        for g, (s, keep) in zip(group, scored):
            s_scr[slot, g] = s
            cm = jnp.max(s, axis=0, keepdims=True)
            keep = ones_row if keep is None else jnp.where(keep, 1.0, 0.0)
            cms.append(jnp.where(keep > 0.5, cm, -jnp.inf))
            keeps.append(keep)
        return tuple(cms), tuple(keeps)

    def step(pos, slot, carry, score_next):
        ms, cms, keeps = carry
        other = 1 - slot
        new_m, next_cm, next_keep = [], [], []
        for group in groups:
            if score_next is not None:
                cm2, keep2 = qk_stage(other, score_next, group)
            else:
                cm2, keep2 = [cms[g] for g in group], [keeps[g] for g in group]
            pvs = [p_times_v(g, jnp.maximum(pos - 1, 0), other) for g in group]
            for g, pv in zip(group, pvs):
                m_new, a = softmax_stage(slot, g, ms[g], cms[g], keeps[g])
                acc_scr[g] = a * (acc_scr[g] + pv)
                new_m.append(m_new)
            next_cm += list(cm2)
            next_keep += list(keep2)
        return tuple(new_m), tuple(next_cm), tuple(next_keep)

    for g in heads:
        p_scr[1, g] = jnp.zeros(p_scr.shape[2:], BF16)
        acc_scr[g] = jnp.zeros(acc_scr.shape[1:], F32)
    cm0, keep0 = [], []
    for group in groups:
        cm2, keep2 = qk_stage(0, lambda g: (score_diag(g), None), group)
        cm0 += list(cm2)
        keep0 += list(keep2)
    carry = (tuple(jnp.full((1, width), -jnp.inf, F32) for _ in heads), tuple(cm0), tuple(keep0))

    def pair(u, carry):
        pos = 2 * u
        carry = step(pos, 0, carry, lambda g: score_past(g, pos))
        return step(pos + 1, 1, carry, lambda g: score_past(g, pos + 1))

    carry = lax.fori_loop(0, qi // 2, pair, carry)
    carry = lax.cond(qi % 2 == 1,
                     lambda c: step(qi - 1, 0, c, lambda g: score_past(g, qi - 1)),
                     lambda c: c, carry)
    step(qi, qi % 2, carry, None)
    return [acc_scr[g] + p_times_v(g, qi, qi % 2) for g in heads]


def _top_blocks(gate, n_valid, axis):
    idx = lax.broadcasted_iota(jnp.int32, gate.shape, axis)
    size = gate.shape[axis]
    g = jnp.where(idx < n_valid, gate, -jnp.inf)
    sel = jnp.zeros(gate.shape, F32)
    for r in range(MOBA_TOPK):
        mx = jnp.max(g, axis=axis, keepdims=True)
        pick = jnp.min(jnp.where(g == mx, idx, size), axis=axis, keepdims=True)
        hit = idx == pick
        sel = jnp.maximum(sel, jnp.where(hit, jnp.where(r < n_valid, 1.0, 0.0), 0.0))
        g = jnp.where(hit, -jnp.inf, g)
    return sel


def _head_cols(g):
    return slice(g * HEAD_W, (g + 1) * HEAD_W)


def _diff_attn_kernel(q_ref, k_ref, vt_ref, lp_ref, sg_ref, o_ref, s_scr, p_scr, acc_scr, *,
                      lam_init):
    qi = pl.program_id(2)
    tq = ATT_TILE
    key = lax.broadcasted_iota(jnp.int32, (tq, 2 * tq), 0)
    qcol = lax.broadcasted_iota(jnp.int32, (tq, 2 * tq), 1) % tq
    lane = lax.broadcasted_iota(jnp.int32, (tq, HEAD_W), 1)

    q12 = []
    for g in range(s_scr.shape[1]):
        q = q_ref[:, _head_cols(g)]
        zero = jnp.zeros_like(q)
        q12.append(jnp.concatenate([jnp.where(lane < D_A, q, zero),
                                    jnp.where(lane < D_A, zero, q)], axis=0))

    def scores(g, j):
        off = pl.multiple_of(j * tq, tq)
        return lax.dot_general(k_ref[pl.ds(off, tq), _head_cols(g)], q12[g], _NT,
                               preferred_element_type=F32)

    outs = _flash_heads(qi, lambda g: jnp.where(key <= qcol, scores(g, qi), -jnp.inf),
                        lambda g, j: (scores(g, j), None), vt_ref, s_scr, p_scr, acc_scr)
    lam = _lambda_full(lp_ref, lam_init)
    for g, acc in enumerate(outs):
        o = acc[:HEAD_W, :] / acc[HEAD_W:HEAD_W + 1, :]
        d = (o[:, :tq] - lam * o[:, tq:]).T
        o_ref[:, _head_cols(g)] = _subln(d, sg_ref[...], lam_init)


def _attn_specs(branch, seq, nq, nh):
    q_tile, k_tile = (T_QA, T_KA) if branch == 0 else (T_QB, T_KB)
    groups = N_HEADS // nh
    width = nh * HEAD_W
    return [
        pl.BlockSpec((ATT_TILE, width), lambda b, h, i: (b * nq + i, q_tile * groups + h)),
        pl.BlockSpec((seq, width), lambda b, h, i: (b, k_tile * groups + h)),
        pl.BlockSpec((None, nh, nq, VT_ROWS, ATT_TILE),
                     lambda b, h, i: (b, branch * groups + h, 0, 0, 0)),
    ]


def _flash_scratch(width, nh):
    return [pltpu.VMEM((2, nh, ATT_TILE, width), F32),
            pltpu.VMEM((2, nh, ATT_TILE, width), BF16),
            pltpu.VMEM((nh, VT_ROWS, width), F32)]


def _attn_call(kern, name, branch, nh, extra_specs, scratch, args, *, batch, seq):
    nq = seq // ATT_TILE
    width = nh * HEAD_W
    return pl.pallas_call(
        kern,
        grid=(batch, N_HEADS // nh, nq),
        in_specs=_attn_specs(branch, seq, nq, nh) + extra_specs,
        out_specs=pl.BlockSpec((ATT_TILE, width), lambda b, h, i: (b * nq + i, h)),
        out_shape=jax.ShapeDtypeStruct((batch * seq, BRANCH_W), F32),
        scratch_shapes=scratch,
        compiler_params=pltpu.CompilerParams(
            dimension_semantics=("parallel", "parallel", "arbitrary"),
            vmem_limit_bytes=VMEM_LIMIT),
        name=name,
    )(*args)


def _diff_attn(zb, vt, lam_params, subln_g, *, batch, seq, lam_init):
    kern = functools.partial(_diff_attn_kernel, lam_init=lam_init)
    extra = [pl.BlockSpec((4, D_A), lambda b, h, i: (0, 0)),
             pl.BlockSpec((1, HEAD_W), lambda b, h, i: (0, 0))]
    return _attn_call(kern, "diff_attn", 0, DIFF_HEADS, extra,
                      _flash_scratch(2 * ATT_TILE, DIFF_HEADS),
                      (zb, zb, vt, lam_params, subln_g), batch=batch, seq=seq)


def _moba_attn_kernel(q_ref, k_ref, vt_ref, o_ref, kmean_scr, sel_scr, s_scr, p_scr, acc_scr, *,
                      nb):
    qi = pl.program_id(2)
    tq = ATT_TILE
    scale = HEAD_W ** -0.5 * LOG2E
    heads = range(s_scr.shape[1])

    @pl.when(qi == 0)
    def _():
        kmean_scr[...] = jnp.zeros(kmean_scr.shape, F32)
        for g in heads:
            for j in range(nb):
                blk = k_ref[j * tq:(j + 1) * tq, _head_cols(g)].astype(F32)
                kmean_scr[g, j:j + 1, :] = jnp.mean(blk, axis=0, keepdims=True)

    key = lax.broadcasted_iota(jnp.int32, (tq, tq), 0)
    qcol = lax.broadcasted_iota(jnp.int32, (tq, tq), 1)
    qs = [q_ref[:, _head_cols(g)] for g in heads]

    def scores(g, j):
        off = pl.multiple_of(j * tq, tq)
        return lax.dot_general(k_ref[pl.ds(off, tq), _head_cols(g)], qs[g], _NT,
                               preferred_element_type=F32) * scale

    for g in heads:
        gate = lax.dot_general(kmean_scr[g], qs[g].astype(F32), _NT,
                               precision=lax.Precision.HIGHEST, preferred_element_type=F32)
        sel_scr[g] = _top_blocks(gate, qi, axis=0)

    def score_past(g, j):
        return scores(g, j), sel_scr[g, pl.ds(j, 1), :] > 0.5

    outs = _flash_heads(qi, lambda g: jnp.where(key <= qcol, scores(g, qi), -jnp.inf), score_past,
                        vt_ref, s_scr, p_scr, acc_scr)
    for g, acc in enumerate(outs):
        o_ref[:, _head_cols(g)] = (acc[:HEAD_W, :] / acc[HEAD_W:HEAD_W + 1, :]).T


def _moba_attn(zb, vt, *, batch, seq):
    assert ATT_TILE == MOBA_BLOCK
    nq = seq // ATT_TILE
    nb_pad = -(-nq // 8) * 8
    kern = functools.partial(_moba_attn_kernel, nb=nq)
    scratch = [pltpu.VMEM((MOBA_HEADS, nb_pad, HEAD_W), F32),
               pltpu.VMEM((MOBA_HEADS, nb_pad, ATT_TILE), F32)] + _flash_scratch(ATT_TILE, MOBA_HEADS)
    return _attn_call(kern, "moba_attn", 1, MOBA_HEADS, [], scratch, (zb, zb, vt),
                      batch=batch, seq=seq)


Q_ROWS = 16


def _state_rows(h):
    return slice(Q_ROWS * h, Q_ROWS * (h + 1))


def _cached_rows(kv_refs, gs, kv):
    return jnp.concatenate(
        [kv_refs[g][:, kv].reshape(-1, HEAD_W).astype(BF16) for g in gs], axis=0)


def _own_head_mask():
    row_head = lax.broadcasted_iota(jnp.int32, (N_HEADS * Q_ROWS, LANES), 0) // Q_ROWS
    col_head = lax.broadcasted_iota(jnp.int32, (N_HEADS * Q_ROWS, LANES), 1) % N_HEADS
    return row_head == col_head


def _mask_cols(y, mask, fill):
    return jnp.concatenate(
        [jnp.where(mask, y[:, t * LANES:(t + 1) * LANES], fill) for t in range(y.shape[1] // LANES)],
        axis=1)


def _own_p_times_v(pb, vn_ref):
    return jnp.concatenate(
        [jnp.dot(pb[_state_rows(h), :], _pad_rows(vn_ref[:, _head_cols(h)]).astype(BF16),
                 preferred_element_type=F32) for h in range(N_HEADS)], axis=0)


def _pad_rows(x_f32):
    return jnp.concatenate([x_f32, jnp.zeros((LANES - x_f32.shape[0], x_f32.shape[1]), F32)], axis=0)


def _head_queries(q_ref, h, halves, q_scale=1.0):
    q = q_ref[:, HEAD_W * h:HEAD_W * (h + 1)]
    zero = jnp.zeros_like(q)
    if halves:
        lane = lax.broadcasted_iota(jnp.int32, q.shape, 1)
        q16 = jnp.concatenate([jnp.where(lane < D_A, q, zero), jnp.where(lane < D_A, zero, q)], axis=0)
    else:
        q16 = jnp.concatenate([q * q_scale, zero], axis=0)
    return q16.astype(BF16)


def _own_scores(q16, kn_ref, h, dec):
    kn = _pad_rows(kn_ref[:, HEAD_W * h:HEAD_W * (h + 1)]).astype(BF16)
    s = lax.dot_general(q16, kn, _NT, preferred_element_type=F32)
    key = lax.broadcasted_iota(jnp.int32, s.shape, 1)
    t = lax.broadcasted_iota(jnp.int32, s.shape, 0) % dec
    return s, key <= t


def _diff_dec_kernel(pt_ref, q_ref, kn_ref, vn_ref, lp_ref, sg_ref, *rest, pages, lam_init):
    kv_refs = rest[:pages]
    o_ref = rest[pages]
    q_scr, m_scr, l_scr, acc_scr = rest[pages + 1:]
    c = pl.program_id(1)
    dec = q_ref.shape[0]

    @pl.when(c == 0)
    def _():
        for h in range(N_HEADS):
            q_scr[_state_rows(h), :] = _head_queries(q_ref, h, halves=True)
        m_scr[...] = jnp.full(m_scr.shape, -jnp.inf, F32)
        l_scr[...] = jnp.zeros(l_scr.shape, F32)
        acc_scr[...] = jnp.zeros(acc_scr.shape, F32)

    def update(s, p_times_v):
        m_old = m_scr[...]
        m_new = jnp.maximum(m_old, jnp.max(s, axis=1, keepdims=True))
        alpha = jnp.exp2(m_old - m_new)
        p = jnp.exp2(s - m_new[:, :1])
        l_scr[...] = alpha * l_scr[...] + jnp.sum(p, axis=1, keepdims=True)
        m_scr[...] = m_new
        acc_scr[...] = alpha * acc_scr[...] + p_times_v(p.astype(BF16))

    own_head = _own_head_mask()
    for lo in range(0, pages, DEC_SPAN):
        halves = [range(lo, lo + DEC_SPAN // 2), range(lo + DEC_SPAN // 2, lo + DEC_SPAN)]
        ys = [lax.dot_general(q_scr[...], _cached_rows(kv_refs, gs, 0), _NT,
                              preferred_element_type=F32) for gs in halves]
        for gs, y in zip(halves, ys):
            v_all = _cached_rows(kv_refs, gs, 1)
            update(_mask_cols(y, own_head, -jnp.inf),
                   lambda pb: jnp.dot(pb, v_all, preferred_element_type=F32))

    @pl.when(c == pl.num_programs(1) - 1)
    def _():
        owns = [_own_scores(q_scr[_state_rows(h), :], kn_ref, h, dec) for h in range(N_HEADS)]
        s = jnp.concatenate([jnp.where(ok, s_h, -jnp.inf) for s_h, ok in owns], axis=0)
        update(s, lambda pb: _own_p_times_v(pb, vn_ref))
        o = acc_scr[...] / l_scr[...]
        lam = _lambda_full(lp_ref, lam_init)
        for h in range(N_HEADS):
            r = Q_ROWS * h
            d = o[r:r + dec, :] - lam * o[r + 8:r + 8 + dec, :]
            o_ref[:, _head_cols(h)] = _subln(d, sg_ref[...], lam_init)


def _dec_specs(dec, page, pages, layer, tiles):
    row_specs = [pl.BlockSpec((dec, BRANCH_W), lambda b, c, pt, tile=tile: (b, tile))
                 for tile in tiles]
    cache_specs = [
        pl.BlockSpec((None, None, page, 2, N_HEADS, HEAD_W),
                     lambda b, c, pt, g=g: (layer, pt[b, c * pages + g], 0, 0, 0, 0))
        for g in range(pages)]
    return row_specs, cache_specs


def _diff_dec(page_table, zs, cache, lam_params, subln_g, *, layer, dec, pages, lam_init):
    n_seq, n_pages = page_table.shape
    page = cache.shape[2]
    assert dec == 8 and n_pages % pages == 0 and pages % DEC_SPAN == 0
    kern = functools.partial(_diff_dec_kernel, pages=pages, lam_init=lam_init)
    row_specs, cache_specs = _dec_specs(dec, page, pages, layer, (T_QA, T_KA, T_VA))
    state = pltpu.VMEM((N_HEADS * Q_ROWS, LANES), F32)
    return pl.pallas_call(
        kern,
        grid_spec=pltpu.PrefetchScalarGridSpec(
            num_scalar_prefetch=1,
            grid=(n_seq, n_pages // pages),
            in_specs=row_specs + [pl.BlockSpec((4, D_A), lambda b, c, pt: (0, 0)),
                                  pl.BlockSpec((1, HEAD_W), lambda b, c, pt: (0, 0))] + cache_specs,
            out_specs=pl.BlockSpec((dec, BRANCH_W), lambda b, c, pt: (b, 0)),
            scratch_shapes=[pltpu.VMEM((N_HEADS * Q_ROWS, HEAD_W), BF16), state, state, state]),
        out_shape=jax.ShapeDtypeStruct((n_seq * dec, BRANCH_W), F32),
        compiler_params=pltpu.CompilerParams(
            dimension_semantics=("parallel", "arbitrary"), vmem_limit_bytes=VMEM_LIMIT),
        name="diff_dec",
    )(page_table, zs, zs, zs, lam_params, subln_g, *([cache] * pages))


def _moba_dec_kernel(pt_ref, q_ref, kn_ref, vn_ref, *rest, pages, page, n_blocks):
    kv_refs = rest[:pages]
    o_ref = rest[pages]
    q_scr, gate_scr, m_scr, l_scr, acc_scr = rest[pages + 1:]
    c = pl.program_id(1)
    dec = q_ref.shape[0]
    scale = HEAD_W ** -0.5 * LOG2E
    pages_per_block = MOBA_BLOCK // page
    blocks_per_step = pages // pages_per_block
    blk_lane = lax.broadcasted_iota(jnp.int32, (N_HEADS * Q_ROWS, LANES), 1)

    @pl.when(c == 0)
    def _():
        for h in range(N_HEADS):
            q_scr[_state_rows(h), :] = _head_queries(q_ref, h, halves=False, q_scale=scale)
        gate_scr[...] = jnp.zeros(gate_scr.shape, F32)
        m_scr[...] = jnp.zeros(m_scr.shape, F32)
        l_scr[...] = jnp.zeros(l_scr.shape, F32)

    def block_softmax(s, p_times_v):
        m = jnp.max(s, axis=1, keepdims=True)
        p = jnp.exp2(s - m)
        return m, jnp.sum(p, axis=1, keepdims=True), p_times_v(p.astype(BF16))

    def block_pages(blk):
        return range(blk * pages_per_block, (blk + 1) * pages_per_block)

    own_head = _own_head_mask()
    span = DEC_SPAN // pages_per_block
    ys = {}
    for blk in range(blocks_per_step):
        if blk % span == 0:
            for b2 in range(blk, blk + span):
                ys[b2] = lax.dot_general(q_scr[...], _cached_rows(kv_refs, block_pages(b2), 0),
                                         _NT, preferred_element_type=F32)
        j = c * blocks_per_step + blk
        here = blk_lane == j
        gate = jnp.sum(_mask_cols(ys[blk], own_head, 0.0), axis=1, keepdims=True) * (
            1.0 / (MOBA_BLOCK * scale))
        v_all = _cached_rows(kv_refs, block_pages(blk), 1)
        m, l, acc = block_softmax(_mask_cols(ys[blk], own_head, -jnp.inf),
                                  lambda pb: jnp.dot(pb, v_all, preferred_element_type=F32))
        gate_scr[...] = jnp.where(here, gate, gate_scr[...])
        m_scr[...] = jnp.where(here, m, m_scr[...])
        l_scr[...] = jnp.where(here, l, l_scr[...])
        acc_scr[j] = acc

    @pl.when(c == pl.num_programs(1) - 1)
    def _():
        owns = [_own_scores(q_scr[_state_rows(h), :], kn_ref, h, dec) for h in range(N_HEADS)]
        s = jnp.concatenate([jnp.where(ok, s_h, -jnp.inf) for s_h, ok in owns], axis=0)
        m_own, l_own, acc_own = block_softmax(s, lambda pb: _own_p_times_v(pb, vn_ref))
        picked = _top_blocks(gate_scr[...], n_blocks, axis=1) > 0.5
        m_all = m_scr[...]
        m_fin = jnp.maximum(m_own, jnp.max(jnp.where(picked, m_all, -jnp.inf),
                                           axis=1, keepdims=True))
        w = jnp.where(picked, jnp.exp2(m_all - m_fin), 0.0)
        w_own = jnp.exp2(m_own - m_fin)
        l_fin = w_own * l_own + jnp.sum(w * l_scr[...], axis=1, keepdims=True)
        acc = w_own * acc_own
        for jb in range(n_blocks):
            acc = acc + w[:, jb:jb + 1] * acc_scr[jb]
        o = acc / l_fin
        for h in range(N_HEADS):
            o_ref[:, _head_cols(h)] = o[Q_ROWS * h:Q_ROWS * h + dec, :]


def _moba_dec(page_table, zs, cache, *, layer, dec, pages):
    n_seq, n_pages = page_table.shape
    page = cache.shape[2]
    past = n_pages * page
    assert dec == 8 and past % MOBA_BLOCK == 0 and MOBA_BLOCK % page == 0
    assert DEC_SPAN % (MOBA_BLOCK // page) == 0 and pages % DEC_SPAN == 0 and n_pages % pages == 0
    n_blocks = past // MOBA_BLOCK
    assert MOBA_TOPK <= n_blocks <= LANES
    kern = functools.partial(_moba_dec_kernel, pages=pages, page=page, n_blocks=n_blocks)
    row_specs, cache_specs = _dec_specs(dec, page, pages, layer, (T_QB, T_KB, T_VB))
    stat = pltpu.VMEM((N_HEADS * Q_ROWS, LANES), F32)
    return pl.pallas_call(
        kern,
        grid_spec=pltpu.PrefetchScalarGridSpec(
            num_scalar_prefetch=1,
            grid=(n_seq, n_pages // pages),
            in_specs=row_specs + cache_specs,
            out_specs=pl.BlockSpec((dec, BRANCH_W), lambda b, c, pt: (b, 0)),
            scratch_shapes=[pltpu.VMEM((N_HEADS * Q_ROWS, HEAD_W), BF16), stat, stat, stat,
                            pltpu.VMEM((n_blocks, N_HEADS * Q_ROWS, HEAD_W), F32)]),
        out_shape=jax.ShapeDtypeStruct((n_seq * dec, BRANCH_W), F32),
        compiler_params=pltpu.CompilerParams(
            dimension_semantics=("parallel", "arbitrary"), vmem_limit_bytes=VMEM_LIMIT),
        name="moba_dec",
    )(page_table, zs, zs, zs, *([cache] * pages))


def _out_kernel(x_ref, oa_ref, ob_ref, ga_ref, gb_ref, gm_ref, wa_ref, wb_ref, wo_ref, y_ref):
    d_model = x_ref.shape[1]
    ua = (oa_ref[...] * ga_ref[...].astype(F32)).astype(BF16)
    ub = (ob_ref[...] * gb_ref[...].astype(F32)).astype(BF16)
    ya = jnp.dot(ua, wa_ref[...], preferred_element_type=F32)
    yb = jnp.dot(ub, wb_ref[...], preferred_element_type=F32)
    merged = (gm_ref[:, :d_model].astype(F32) * ya + gm_ref[:, d_model:].astype(F32) * yb)
    y_ref[...] = x_ref[...] + jnp.dot(merged.astype(BF16), wo_ref[...],
                                      preferred_element_type=F32)


def _out(x, oa, ob, zb, wa, wb, wo, *, layer, tm):
    rows, d_model = x.shape
    assert 2 * d_model == 4 * BRANCH_W
    const = lambda shape: pl.BlockSpec((None,) + shape, lambda m: (layer, 0, 0),
                                       pipeline_mode=pl.Buffered(1))
    return pl.pallas_call(
        _out_kernel,
        grid=(rows // tm,),
        in_specs=[
            pl.BlockSpec((tm, d_model), lambda m: (m, 0)),
            pl.BlockSpec((tm, BRANCH_W), lambda m: (m, 0)),
            pl.BlockSpec((tm, BRANCH_W), lambda m: (m, 0)),
            pl.BlockSpec((tm, BRANCH_W), lambda m: (m, T_GA)),
            pl.BlockSpec((tm, BRANCH_W), lambda m: (m, T_GB)),
            pl.BlockSpec((tm, 2 * d_model), lambda m: (m, T_GM * BRANCH_W // (2 * d_model))),
            const((BRANCH_W, d_model)),
            const((BRANCH_W, d_model)),
            const((d_model, d_model)),
        ],
        out_specs=pl.BlockSpec((tm, d_model), lambda m: (m, 0)),
        out_shape=jax.ShapeDtypeStruct((rows, d_model), F32),
        compiler_params=pltpu.CompilerParams(
            dimension_semantics=("parallel",), vmem_limit_bytes=VMEM_LIMIT),
        name="out",
    )(x, oa, ob, zb, zb, zb, wa, wb, wo)


def _rope_table(pos):
    pos = pos.astype(F32)[:, None]
    parts = []
    for d in (D_A, HEAD_W):
        half = d // 2
        inv = ROPE_THETA ** (-2.0 * jnp.arange(half, dtype=F32) / d)
        ang = pos * inv[None, :]
        cos, sin = jnp.cos(ang), jnp.sin(ang)
        reps = LANES // d
        parts.append(jnp.tile(jnp.concatenate([cos, cos], axis=-1), (1, reps)))
        parts.append(jnp.tile(jnp.concatenate([-sin, sin], axis=-1), (1, reps)))
    return jnp.concatenate(parts, axis=-1)


def _aux_rows(q_norm_a, k_norm_a, q_norm_b, k_norm_b, b_merge):
    zero = jnp.zeros((BRANCH_W,), F32)
    rows = [jnp.tile(q_norm_a, BRANCH_W // D_A), jnp.tile(k_norm_a, BRANCH_W // D_A), zero, zero,
            jnp.tile(q_norm_b, BRANCH_W // HEAD_W), jnp.tile(k_norm_b, BRANCH_W // HEAD_W),
            zero, zero]
    rows += list(b_merge.reshape(-1, BRANCH_W))
    return jnp.stack(rows)[:, None, :]


def kernel(x_prompt, x_sample, cache_kv_diff, cache_kv_moba, page_table, norm_gain, w_in, q_norm_a, k_norm_a, lambda_q1, lambda_k1, lambda_q2, lambda_k2, subln_gain, q_norm_b, k_norm_b, b_merge, w_up_a, w_up_b, w_out):
    batch, seq, d_model = x_prompt.shape
    n_seq, dec, _ = x_sample.shape
    depth, n_pool, page = cache_kv_diff.shape[:3]
    n_pages = page_table.shape[1]
    past = n_pages * page
    assert w_in.shape[2] == N_TILES * BRANCH_W and seq % MOBA_BLOCK == 0

    cache_d, cache_m = cache_kv_diff, cache_kv_moba
    assert cache_d.shape[3:] == cache_m.shape[3:] == (2, N_HEADS, HEAD_W)
    tab_p = _rope_table(jnp.arange(seq))
    tab_s = jnp.tile(_rope_table(past + jnp.arange(dec)), (n_seq, 1))

    yp = x_prompt.reshape(batch * seq, d_model)
    ys = x_sample.reshape(n_seq * dec, d_model)
    kv_p = kv_s = None
    w_in_l = w_in.astype(BF16)
    wa, wb, wo = w_up_a.astype(BF16), w_up_b.astype(BF16), w_out.astype(BF16)
    for l in range(depth):
        lam_init = _lambda_init(l)
        g = norm_gain[l][None, :]
        aux = _aux_rows(q_norm_a[l], k_norm_a[l], q_norm_b[l], k_norm_b[l], b_merge[l])
        lam_params = jnp.stack([lambda_q1[l], lambda_k1[l], lambda_q2[l], lambda_k2[l]])
        sg = subln_gain[l][None, :]

        zb, *kv_p, vt = _proj(yp, g, w_in_l, tab_p, aux, kv_p, layer=l, depth=depth, tm=512,
                              z_dtype=BF16, vt_seq=seq)
        oa = _diff_attn(zb, vt, lam_params, sg, batch=batch, seq=seq, lam_init=lam_init)
        ob = _moba_attn(zb, vt, batch=batch, seq=seq)
        yp = _out(yp, oa, ob, zb, wa, wb, wo, layer=l, tm=256)

        zs, *kv_s = _proj(ys, g, w_in_l, tab_s, aux, kv_s, layer=l, depth=depth, tm=n_seq * dec,
                          z_dtype=F32)
        oa = _diff_dec(page_table, zs, cache_d, lam_params, sg, layer=l, dec=dec, pages=DEC_PAGES,
                       lam_init=lam_init)
        ob = _moba_dec(page_table, zs, cache_m, layer=l, dec=dec, pages=DEC_PAGES)
        ys = _out(ys, oa, ob, zs, wa, wb, wo, layer=l, tm=n_seq * dec)

    kv_shape = lambda rows_a, rows_b: (depth, rows_a, rows_b, 2, N_HEADS, HEAD_W)
    return (yp.reshape(batch, seq, d_model),
            ys.reshape(n_seq, dec, d_model),
            kv_p[0].reshape(kv_shape(batch, seq)),
            kv_p[1].reshape(kv_shape(batch, seq)),
            kv_s[0].reshape(kv_shape(n_seq, dec)),
            kv_s[1].reshape(kv_shape(n_seq, dec)))
```

```python
import functools
import math

import jax
import jax.numpy as jnp
from jax import lax
from jax.experimental import pallas as pl
from jax.experimental.pallas import tpu as pltpu

F32 = jnp.float32
BF16 = jnp.bfloat16

ROPE_THETA = 10000.0
NORM_EPS = 1e-6
N_HEADS = 8
HEAD_W = 128
BRANCH_W = N_HEADS * HEAD_W
D_A = 64
MOBA_BLOCK = 256
MOBA_TOPK = 3
LANES = 128
ATT_TILE = 256
PROJ_SUB = 256
LOG2E = math.log2(math.e)
VT_ONES = 16
VT_ROWS = HEAD_W + VT_ONES
DIFF_HEADS = 4
MOBA_HEADS = 4
MXU_GROUP_COLS = 512
DEC_SPAN = 8
DEC_PAGES = 16
KV_ROWS = 2 * N_HEADS
VMEM_LIMIT = 56 * 1024 * 1024

T_QA, T_KA, T_VA, T_GA, T_QB, T_KB, T_VB, T_GB, T_GM = 0, 1, 2, 3, 4, 5, 6, 7, 8
N_TILES = 12

_NT = (((1,), (1,)), ((), ()))


def _lambda_init(layer):
    return 0.8 - 0.6 * math.exp(-0.3 * layer)


def _norm_rope(x, gain, cos, sin, d):
    lane = lax.broadcasted_iota(jnp.int32, x.shape, 1)
    x2 = x * x
    if d == 64:
        lo = lane < 64
        s_lo = jnp.sum(jnp.where(lo, x2, 0.0), axis=-1, keepdims=True)
        s_hi = jnp.sum(jnp.where(lo, 0.0, x2), axis=-1, keepdims=True)
        ms = jnp.where(lo, s_lo, s_hi) * (1.0 / 64.0)
    else:
        ms = jnp.sum(x2, axis=-1, keepdims=True) * (1.0 / 128.0)
    y = x * lax.rsqrt(ms + NORM_EPS) * gain
    if d == 64:
        partner = jnp.where((lane & 63) < 32, pltpu.roll(y, 96, 1), pltpu.roll(y, 32, 1))
    else:
        partner = pltpu.roll(y, 64, 1)
    return y * cos + partner * sin


def _proj_kernel(*refs, emit_vt, aliased):
    x_ref, g_ref, w_ref, tab_ref, aux_ref = refs[:5]
    outs = refs[5 + (2 if aliased else 0):]
    zb_ref, kvd_ref, kvm_ref = outs[:3]
    vt_ref = outs[3] if emit_vt else None
    h_scr = outs[-1]
    n = pl.program_id(1)

    @pl.when(n == 0)
    def _():
        x = x_ref[...]
        ms = jnp.mean(x * x, axis=-1, keepdims=True)
        h_scr[...] = (x * lax.rsqrt(ms + NORM_EPS) * g_ref[...]).astype(BF16)

    aux = aux_ref[0]
    tm = h_scr.shape[0]
    sub = min(tm, PROJ_SUB)

    def row_blocks():
        for r in range(tm // sub):
            rows = slice(r * sub, (r + 1) * sub)
            yield r, rows, jnp.dot(h_scr[rows, :], w_ref[...], preferred_element_type=F32)

    def put_norm_rope(d, q_scale, f32_ref):
        base = 0 if d == 64 else 2 * LANES
        for _, rows, z in row_blocks():
            cos = tab_ref[rows, base:base + LANES]
            sin = tab_ref[rows, base + LANES:base + 2 * LANES]
            for c in range(N_HEADS):
                sl = slice(c * HEAD_W, (c + 1) * HEAD_W)
                r = _norm_rope(z[:, sl], aux[:, sl], cos, sin, d)
                if f32_ref is not None:
                    f32_ref[rows, c, :] = r
                zb_ref[rows, sl] = (r * q_scale).astype(zb_ref.dtype)

    def put_values(f32_ref):
        for r, rows, z in row_blocks():
            for c in range(N_HEADS):
                zc = z[:, c * HEAD_W:(c + 1) * HEAD_W]
                f32_ref[rows, c, :] = zc
                if emit_vt:
                    for u in range(sub // LANES):
                        chunk, col = divmod(r * sub + u * LANES, ATT_TILE)
                        vt_ref[c, chunk, :HEAD_W, col:col + LANES] = (
                            zc[u * LANES:(u + 1) * LANES, :].T.astype(BF16))
                        if col == 0:
                            vt_ref[c, chunk, HEAD_W:, :] = jnp.ones((VT_ONES, ATT_TILE), BF16)
            zb_ref[rows, :] = z.astype(zb_ref.dtype)

    def put_silu():
        for _, rows, z in row_blocks():
            zb_ref[rows, :] = (z * jax.nn.sigmoid(z)).astype(zb_ref.dtype)

    def put_sigmoid():
        for _, rows, z in row_blocks():
            zb_ref[rows, :] = jax.nn.sigmoid(z + aux).astype(zb_ref.dtype)

    pl.when(n == T_QA)(lambda: put_norm_rope(64, D_A ** -0.5 * LOG2E, None))
    pl.when(n == T_KA)(lambda: put_norm_rope(64, 1.0, kvd_ref))
    pl.when(n == T_VA)(lambda: put_values(kvd_ref))
    pl.when(n == T_QB)(lambda: put_norm_rope(128, 1.0, None))
    pl.when(n == T_KB)(lambda: put_norm_rope(128, 1.0, kvm_ref))
    pl.when(n == T_VB)(lambda: put_values(kvm_ref))
    pl.when((n == T_GA) | (n == T_GB))(put_silu)
    pl.when(n >= T_GM)(put_sigmoid)


def _proj(x, g, w_bf, tab, aux, kv_bufs, *, layer, depth, tm, z_dtype, vt_seq=None):
    rows, d_model = x.shape
    tab_blocks = tab.shape[0] // tm
    emit_vt = vt_seq is not None
    aliased = kv_bufs is not None
    clip01 = lambda v: jnp.minimum(jnp.maximum(v, 0), 1)
    in_specs = [
        pl.BlockSpec((tm, d_model), lambda m, n: (m, 0)),
        pl.BlockSpec((1, d_model), lambda m, n: (0, 0)),
        pl.BlockSpec((None, d_model, BRANCH_W), lambda m, n: (layer, 0, n)),
        pl.BlockSpec((tm, 4 * LANES), lambda m, n: (m % tab_blocks, 0)),
        pl.BlockSpec((1, 1, BRANCH_W), lambda m, n: (n, 0, 0)),
    ]
    args = [x, g, w_bf, tab, aux]
    if aliased:
        in_specs += [pl.BlockSpec(memory_space=pl.ANY)] * 2
        args += list(kv_bufs)
    kv_sds = jax.ShapeDtypeStruct((depth, rows, KV_ROWS, HEAD_W), F32)
    out_specs = [
        pl.BlockSpec((tm, BRANCH_W), lambda m, n: (m, n)),
        pl.BlockSpec((None, tm, N_HEADS, HEAD_W), lambda m, n: (layer, m, clip01(n - T_KA), 0)),
        pl.BlockSpec((None, tm, N_HEADS, HEAD_W), lambda m, n: (layer, m, clip01(n - T_KB), 0)),
    ]
    out_shape = [jax.ShapeDtypeStruct((rows, N_TILES * BRANCH_W), z_dtype), kv_sds, kv_sds]
    if emit_vt:
        assert vt_seq % tm == 0 and tm % ATT_TILE == 0
        mpb = vt_seq // tm
        out_specs.append(pl.BlockSpec(
            (None, N_HEADS, tm // ATT_TILE, VT_ROWS, ATT_TILE),
            lambda m, n: (m // mpb, jnp.where(n >= T_VB, 1, 0), m % mpb, 0, 0)))
        out_shape.append(jax.ShapeDtypeStruct(
            (rows // vt_seq, 2 * N_HEADS, vt_seq // ATT_TILE, VT_ROWS, ATT_TILE), BF16))
    return pl.pallas_call(
        functools.partial(_proj_kernel, emit_vt=emit_vt, aliased=aliased),
        grid=(rows // tm, N_TILES),
        in_specs=in_specs,
        out_specs=out_specs,
        out_shape=out_shape,
        input_output_aliases={5: 1, 6: 2} if aliased else {},
        scratch_shapes=[pltpu.VMEM((tm, d_model), BF16)],
        compiler_params=pltpu.CompilerParams(
            dimension_semantics=("parallel", "arbitrary"), vmem_limit_bytes=VMEM_LIMIT),
        name="proj",
    )(*args)


def _lambda_full(lp_ref, lam_init):
    a = jnp.sum(lp_ref[0:1, :] * lp_ref[1:2, :], axis=-1, keepdims=True)
    b = jnp.sum(lp_ref[2:3, :] * lp_ref[3:4, :], axis=-1, keepdims=True)
    return jnp.exp(a) - jnp.exp(b) + lam_init


def _subln(o, gain, lam_init):
    ms = jnp.mean(o * o, axis=-1, keepdims=True)
    return o * lax.rsqrt(ms + NORM_EPS) * gain * (1.0 - lam_init)


def _flash_heads(qi, score_diag, score_past, vt_ref, s_scr, p_scr, acc_scr):
    heads = list(range(s_scr.shape[1]))
    width = s_scr.shape[-1]
    per_group = max(1, MXU_GROUP_COLS // width)
    groups = [heads[i:i + per_group] for i in range(0, len(heads), per_group)]
    ones_row = jnp.ones((1, width), F32)

    def chunk_at(pos):
        return jnp.where(pos == 0, qi, pos - 1)

    def p_times_v(g, pos, slot):
        return jnp.dot(vt_ref[g, chunk_at(pos)], p_scr[slot, g], preferred_element_type=F32)

    def softmax_stage(slot, g, m, cm, keep):
        m_new = jnp.maximum(m, cm)
        a = jnp.exp2(m - m_new)
        m_sub = jnp.where(keep > 0.5, m_new, jnp.inf)
        for c in range(width // LANES):
            sl = slice(c * LANES, (c + 1) * LANES)
            p_scr[slot, g, :, sl] = jnp.exp2(s_scr[slot, g, :, sl] - m_sub[:, sl]).astype(BF16)
        return m_new, a

    def qk_stage(slot, score_next, group):
        scored = [score_next(g) for g in group]
        cms, keeps = [], []
        for g, (s, keep) in zip(group, scored):
            s_scr[slot, g] = s
            cm = jnp.max(s, axis=0, keepdims=True)
            keep = ones_row if keep is None else jnp.where(keep, 1.0, 0.0)
            cms.append(jnp.where(keep > 0.5, cm, -jnp.inf))
            keeps.append(keep)
        return tuple(cms), tuple(keeps)

    def step(pos, slot, carry, score_next):
        ms, cms, keeps = carry
        other = 1 - slot
        new_m, next_cm, next_keep = [], [], []
        for group in groups:
            if score_next is not None:
                cm2, keep2 = qk_stage(other, score_next, group)
            else:
                cm2, keep2 = [cms[g] for g in group], [keeps[g] for g in group]
            pvs = [p_times_v(g, jnp.maximum(pos - 1, 0), other) for g in group]
            for g, pv in zip(group, pvs):
                m_new, a = softmax_stage(slot, g, ms[g], cms[g], keeps[g])
                acc_scr[g] = a * (acc_scr[g] + pv)
                new_m.append(m_new)
            next_cm += list(cm2)
            next_keep += list(keep2)
        return tuple(new_m), tuple(next_cm), tuple(next_keep)

    for g in heads:
        p_scr[1, g] = jnp.zeros(p_scr.shape[2:], BF16)
        acc_scr[g] = jnp.zeros(acc_scr.shape[1:], F32)
    cm0, keep0 = [], []
    for group in groups:
        cm2, keep2 = qk_stage(0, lambda g: (score_diag(g), None), group)
        cm0 += list(cm2)
        keep0 += list(keep2)
    carry = (tuple(jnp.full((1, width), -jnp.inf, F32) for _ in heads), tuple(cm0), tuple(keep0))

    def pair(u, carry):
        pos = 2 * u
        carry = step(pos, 0, carry, lambda g: score_past(g, pos))
        return step(pos + 1, 1, carry, lambda g: score_past(g, pos + 1))

    carry = lax.fori_loop(0, qi // 2, pair, carry)
    carry = lax.cond(qi % 2 == 1,
                     lambda c: step(qi - 1, 0, c, lambda g: score_past(g, qi - 1)),
                     lambda c: c, carry)
    step(qi, qi % 2, carry, None)
    return [acc_scr[g] + p_times_v(g, qi, qi % 2) for g in heads]


def _top_blocks(gate, n_valid, axis):
    idx = lax.broadcasted_iota(jnp.int32, gate.shape, axis)
    size = gate.shape[axis]
    g = jnp.where(idx < n_valid, gate, -jnp.inf)
    sel = jnp.zeros(gate.shape, F32)
    for r in range(MOBA_TOPK):
        mx = jnp.max(g, axis=axis, keepdims=True)
        pick = jnp.min(jnp.where(g == mx, idx, size), axis=axis, keepdims=True)
        hit = idx == pick
        sel = jnp.maximum(sel, jnp.where(hit, jnp.where(r < n_valid, 1.0, 0.0), 0.0))
        g = jnp.where(hit, -jnp.inf, g)
    return sel


def _head_cols(g):
    return slice(g * HEAD_W, (g + 1) * HEAD_W)


def _diff_attn_kernel(q_ref, k_ref, vt_ref, lp_ref, sg_ref, o_ref, s_scr, p_scr, acc_scr, *,
                      lam_init):
    qi = pl.program_id(2)
    tq = ATT_TILE
    key = lax.broadcasted_iota(jnp.int32, (tq, 2 * tq), 0)
    qcol = lax.broadcasted_iota(jnp.int32, (tq, 2 * tq), 1) % tq
    lane = lax.broadcasted_iota(jnp.int32, (tq, HEAD_W), 1)

    q12 = []
    for g in range(s_scr.shape[1]):
        q = q_ref[:, _head_cols(g)]
        zero = jnp.zeros_like(q)
        q12.append(jnp.concatenate([jnp.where(lane < D_A, q, zero),
                                    jnp.where(lane < D_A, zero, q)], axis=0))

    def scores(g, j):
        off = pl.multiple_of(j * tq, tq)
        return lax.dot_general(k_ref[pl.ds(off, tq), _head_cols(g)], q12[g], _NT,
                               preferred_element_type=F32)

    outs = _flash_heads(qi, lambda g: jnp.where(key <= qcol, scores(g, qi), -jnp.inf),
                        lambda g, j: (scores(g, j), None), vt_ref, s_scr, p_scr, acc_scr)
    lam = _lambda_full(lp_ref, lam_init)
    for g, acc in enumerate(outs):
        o = acc[:HEAD_W, :] / acc[HEAD_W:HEAD_W + 1, :]
        d = (o[:, :tq] - lam * o[:, tq:]).T
        o_ref[:, _head_cols(g)] = _subln(d, sg_ref[...], lam_init)


def _attn_specs(branch, seq, nq, nh):
    q_tile, k_tile = (T_QA, T_KA) if branch == 0 else (T_QB, T_KB)
    groups = N_HEADS // nh
    width = nh * HEAD_W
    return [
        pl.BlockSpec((ATT_TILE, width), lambda b, h, i: (b * nq + i, q_tile * groups + h)),
        pl.BlockSpec((seq, width), lambda b, h, i: (b, k_tile * groups + h)),
        pl.BlockSpec((None, nh, nq, VT_ROWS, ATT_TILE),
                     lambda b, h, i: (b, branch * groups + h, 0, 0, 0)),
    ]


def _flash_scratch(width, nh):
    return [pltpu.VMEM((2, nh, ATT_TILE, width), F32),
            pltpu.VMEM((2, nh, ATT_TILE, width), BF16),
            pltpu.VMEM((nh, VT_ROWS, width), F32)]


def _attn_call(kern, name, branch, nh, extra_specs, scratch, args, *, batch, seq):
    nq = seq // ATT_TILE
    width = nh * HEAD_W
    return pl.pallas_call(
        kern,
        grid=(batch, N_HEADS // nh, nq),
        in_specs=_attn_specs(branch, seq, nq, nh) + extra_specs,
        out_specs=pl.BlockSpec((ATT_TILE, width), lambda b, h, i: (b * nq + i, h)),
        out_shape=jax.ShapeDtypeStruct((batch * seq, BRANCH_W), F32),
        scratch_shapes=scratch,
        compiler_params=pltpu.CompilerParams(
            dimension_semantics=("parallel", "parallel", "arbitrary"),
            vmem_limit_bytes=VMEM_LIMIT),
        name=name,
    )(*args)


def _diff_attn(zb, vt, lam_params, subln_g, *, batch, seq, lam_init):
    kern = functools.partial(_diff_attn_kernel, lam_init=lam_init)
    extra = [pl.BlockSpec((4, D_A), lambda b, h, i: (0, 0)),
             pl.BlockSpec((1, HEAD_W), lambda b, h, i: (0, 0))]
    return _attn_call(kern, "diff_attn", 0, DIFF_HEADS, extra,
                      _flash_scratch(2 * ATT_TILE, DIFF_HEADS),
                      (zb, zb, vt, lam_params, subln_g), batch=batch, seq=seq)


def _moba_attn_kernel(q_ref, k_ref, vt_ref, o_ref, kmean_scr, sel_scr, s_scr, p_scr, acc_scr, *,
                      nb):
    qi = pl.program_id(2)
    tq = ATT_TILE
    scale = HEAD_W ** -0.5 * LOG2E
    heads = range(s_scr.shape[1])

    @pl.when(qi == 0)
    def _():
        kmean_scr[...] = jnp.zeros(kmean_scr.shape, F32)
        for g in heads:
            for j in range(nb):
                blk = k_ref[j * tq:(j + 1) * tq, _head_cols(g)].astype(F32)
                kmean_scr[g, j:j + 1, :] = jnp.mean(blk, axis=0, keepdims=True)

    key = lax.broadcasted_iota(jnp.int32, (tq, tq), 0)
    qcol = lax.broadcasted_iota(jnp.int32, (tq, tq), 1)
    qs = [q_ref[:, _head_cols(g)] for g in heads]

    def scores(g, j):
        off = pl.multiple_of(j * tq, tq)
        return lax.dot_general(k_ref[pl.ds(off, tq), _head_cols(g)], qs[g], _NT,
                               preferred_element_type=F32) * scale

    for g in heads:
        gate = lax.dot_general(kmean_scr[g], qs[g].astype(F32), _NT,
                               precision=lax.Precision.HIGHEST, preferred_element_type=F32)
        sel_scr[g] = _top_blocks(gate, qi, axis=0)

    def score_past(g, j):
        return scores(g, j), sel_scr[g, pl.ds(j, 1), :] > 0.5

    outs = _flash_heads(qi, lambda g: jnp.where(key <= qcol, scores(g, qi), -jnp.inf), score_past,
                        vt_ref, s_scr, p_scr, acc_scr)
    for g, acc in enumerate(outs):
        o_ref[:, _head_cols(g)] = (acc[:HEAD_W, :] / acc[HEAD_W:HEAD_W + 1, :]).T


def _moba_attn(zb, vt, *, batch, seq):
    assert ATT_TILE == MOBA_BLOCK
    nq = seq // ATT_TILE
    nb_pad = -(-nq // 8) * 8
    kern = functools.partial(_moba_attn_kernel, nb=nq)
    scratch = [pltpu.VMEM((MOBA_HEADS, nb_pad, HEAD_W), F32),
               pltpu.VMEM((MOBA_HEADS, nb_pad, ATT_TILE), F32)] + _flash_scratch(ATT_TILE, MOBA_HEADS)
    return _attn_call(kern, "moba_attn", 1, MOBA_HEADS, [], scratch, (zb, zb, vt),
                      batch=batch, seq=seq)


Q_ROWS = 16


def _state_rows(h):
    return slice(Q_ROWS * h, Q_ROWS * (h + 1))


def _cached_rows(kv_refs, gs, kv):
    return jnp.concatenate(
        [kv_refs[g][:, kv].reshape(-1, HEAD_W).astype(BF16) for g in gs], axis=0)


def _own_head_mask():
    row_head = lax.broadcasted_iota(jnp.int32, (N_HEADS * Q_ROWS, LANES), 0) // Q_ROWS
    col_head = lax.broadcasted_iota(jnp.int32, (N_HEADS * Q_ROWS, LANES), 1) % N_HEADS
    return row_head == col_head


def _mask_cols(y, mask, fill):
    return jnp.concatenate(
        [jnp.where(mask, y[:, t * LANES:(t + 1) * LANES], fill) for t in range(y.shape[1] // LANES)],
        axis=1)


def _own_p_times_v(pb, vn_ref):
    return jnp.concatenate(
        [jnp.dot(pb[_state_rows(h), :], _pad_rows(vn_ref[:, _head_cols(h)]).astype(BF16),
                 preferred_element_type=F32) for h in range(N_HEADS)], axis=0)


def _pad_rows(x_f32):
    return jnp.concatenate([x_f32, jnp.zeros((LANES - x_f32.shape[0], x_f32.shape[1]), F32)], axis=0)


def _head_queries(q_ref, h, halves, q_scale=1.0):
    q = q_ref[:, HEAD_W * h:HEAD_W * (h + 1)]
    zero = jnp.zeros_like(q)
    if halves:
        lane = lax.broadcasted_iota(jnp.int32, q.shape, 1)
        q16 = jnp.concatenate([jnp.where(lane < D_A, q, zero), jnp.where(lane < D_A, zero, q)], axis=0)
    else:
        q16 = jnp.concatenate([q * q_scale, zero], axis=0)
    return q16.astype(BF16)


def _own_scores(q16, kn_ref, h, dec):
    kn = _pad_rows(kn_ref[:, HEAD_W * h:HEAD_W * (h + 1)]).astype(BF16)
    s = lax.dot_general(q16, kn, _NT, preferred_element_type=F32)
    key = lax.broadcasted_iota(jnp.int32, s.shape, 1)
    t = lax.broadcasted_iota(jnp.int32, s.shape, 0) % dec
    return s, key <= t


def _diff_dec_kernel(pt_ref, q_ref, kn_ref, vn_ref, lp_ref, sg_ref, *rest, pages, lam_init):
    kv_refs = rest[:pages]
    o_ref = rest[pages]
    q_scr, m_scr, l_scr, acc_scr = rest[pages + 1:]
    c = pl.program_id(1)
    dec = q_ref.shape[0]

    @pl.when(c == 0)
    def _():
        for h in range(N_HEADS):
            q_scr[_state_rows(h), :] = _head_queries(q_ref, h, halves=True)
        m_scr[...] = jnp.full(m_scr.shape, -jnp.inf, F32)
        l_scr[...] = jnp.zeros(l_scr.shape, F32)
        acc_scr[...] = jnp.zeros(acc_scr.shape, F32)

    def update(s, p_times_v):
        m_old = m_scr[...]
        m_new = jnp.maximum(m_old, jnp.max(s, axis=1, keepdims=True))
        alpha = jnp.exp2(m_old - m_new)
        p = jnp.exp2(s - m_new[:, :1])
        l_scr[...] = alpha * l_scr[...] + jnp.sum(p, axis=1, keepdims=True)
        m_scr[...] = m_new
        acc_scr[...] = alpha * acc_scr[...] + p_times_v(p.astype(BF16))

    own_head = _own_head_mask()
    for lo in range(0, pages, DEC_SPAN):
        halves = [range(lo, lo + DEC_SPAN // 2), range(lo + DEC_SPAN // 2, lo + DEC_SPAN)]
        ys = [lax.dot_general(q_scr[...], _cached_rows(kv_refs, gs, 0), _NT,
                              preferred_element_type=F32) for gs in halves]
        for gs, y in zip(halves, ys):
            v_all = _cached_rows(kv_refs, gs, 1)
            update(_mask_cols(y, own_head, -jnp.inf),
                   lambda pb: jnp.dot(pb, v_all, preferred_element_type=F32))

    @pl.when(c == pl.num_programs(1) - 1)
    def _():
        owns = [_own_scores(q_scr[_state_rows(h), :], kn_ref, h, dec) for h in range(N_HEADS)]
        s = jnp.concatenate([jnp.where(ok, s_h, -jnp.inf) for s_h, ok in owns], axis=0)
        update(s, lambda pb: _own_p_times_v(pb, vn_ref))
        o = acc_scr[...] / l_scr[...]
        lam = _lambda_full(lp_ref, lam_init)
        for h in range(N_HEADS):
            r = Q_ROWS * h
            d = o[r:r + dec, :] - lam * o[r + 8:r + 8 + dec, :]
            o_ref[:, _head_cols(h)] = _subln(d, sg_ref[...], lam_init)


def _dec_specs(dec, page, pages, layer, tiles):
    row_specs = [pl.BlockSpec((dec, BRANCH_W), lambda b, c, pt, tile=tile: (b, tile))
                 for tile in tiles]
    cache_specs = [
        pl.BlockSpec((None, None, page, 2, N_HEADS, HEAD_W),
                     lambda b, c, pt, g=g: (layer, pt[b, c * pages + g], 0, 0, 0, 0))
        for g in range(pages)]
    return row_specs, cache_specs


def _diff_dec(page_table, zs, cache, lam_params, subln_g, *, layer, dec, pages, lam_init):
    n_seq, n_pages = page_table.shape
    page = cache.shape[2]
    assert dec == 8 and n_pages % pages == 0 and pages % DEC_SPAN == 0
    kern = functools.partial(_diff_dec_kernel, pages=pages, lam_init=lam_init)
    row_specs, cache_specs = _dec_specs(dec, page, pages, layer, (T_QA, T_KA, T_VA))
    state = pltpu.VMEM((N_HEADS * Q_ROWS, LANES), F32)
    return pl.pallas_call(
        kern,
        grid_spec=pltpu.PrefetchScalarGridSpec(
            num_scalar_prefetch=1,
            grid=(n_seq, n_pages // pages),
            in_specs=row_specs + [pl.BlockSpec((4, D_A), lambda b, c, pt: (0, 0)),
                                  pl.BlockSpec((1, HEAD_W), lambda b, c, pt: (0, 0))] + cache_specs,
            out_specs=pl.BlockSpec((dec, BRANCH_W), lambda b, c, pt: (b, 0)),
            scratch_shapes=[pltpu.VMEM((N_HEADS * Q_ROWS, HEAD_W), BF16), state, state, state]),
        out_shape=jax.ShapeDtypeStruct((n_seq * dec, BRANCH_W), F32),
        compiler_params=pltpu.CompilerParams(
            dimension_semantics=("parallel", "arbitrary"), vmem_limit_bytes=VMEM_LIMIT),
        name="diff_dec",
    )(page_table, zs, zs, zs, lam_params, subln_g, *([cache] * pages))


def _moba_dec_kernel(pt_ref, q_ref, kn_ref, vn_ref, *rest, pages, page, n_blocks):
    kv_refs = rest[:pages]
    o_ref = rest[pages]
    q_scr, gate_scr, m_scr, l_scr, acc_scr = rest[pages + 1:]
    c = pl.program_id(1)
    dec = q_ref.shape[0]
    scale = HEAD_W ** -0.5 * LOG2E
    pages_per_block = MOBA_BLOCK // page
    blocks_per_step = pages // pages_per_block
    blk_lane = lax.broadcasted_iota(jnp.int32, (N_HEADS * Q_ROWS, LANES), 1)

    @pl.when(c == 0)
    def _():
        for h in range(N_HEADS):
            q_scr[_state_rows(h), :] = _head_queries(q_ref, h, halves=False, q_scale=scale)
        gate_scr[...] = jnp.zeros(gate_scr.shape, F32)
        m_scr[...] = jnp.zeros(m_scr.shape, F32)
        l_scr[...] = jnp.zeros(l_scr.shape, F32)

    def block_softmax(s, p_times_v):
        m = jnp.max(s, axis=1, keepdims=True)
        p = jnp.exp2(s - m)
        return m, jnp.sum(p, axis=1, keepdims=True), p_times_v(p.astype(BF16))

    def block_pages(blk):
        return range(blk * pages_per_block, (blk + 1) * pages_per_block)

    own_head = _own_head_mask()
    span = DEC_SPAN // pages_per_block
    ys = {}
    for blk in range(blocks_per_step):
        if blk % span == 0:
            for b2 in range(blk, blk + span):
                ys[b2] = lax.dot_general(q_scr[...], _cached_rows(kv_refs, block_pages(b2), 0),
                                         _NT, preferred_element_type=F32)
        j = c * blocks_per_step + blk
        here = blk_lane == j
        gate = jnp.sum(_mask_cols(ys[blk], own_head, 0.0), axis=1, keepdims=True) * (
            1.0 / (MOBA_BLOCK * scale))
        v_all = _cached_rows(kv_refs, block_pages(blk), 1)
        m, l, acc = block_softmax(_mask_cols(ys[blk], own_head, -jnp.inf),
                                  lambda pb: jnp.dot(pb, v_all, preferred_element_type=F32))
        gate_scr[...] = jnp.where(here, gate, gate_scr[...])
        m_scr[...] = jnp.where(here, m, m_scr[...])
        l_scr[...] = jnp.where(here, l, l_scr[...])
        acc_scr[j] = acc

    @pl.when(c == pl.num_programs(1) - 1)
    def _():
        owns = [_own_scores(q_scr[_state_rows(h), :], kn_ref, h, dec) for h in range(N_HEADS)]
        s = jnp.concatenate([jnp.where(ok, s_h, -jnp.inf) for s_h, ok in owns], axis=0)
        m_own, l_own, acc_own = block_softmax(s, lambda pb: _own_p_times_v(pb, vn_ref))
        picked = _top_blocks(gate_scr[...], n_blocks, axis=1) > 0.5
        m_all = m_scr[...]
        m_fin = jnp.maximum(m_own, jnp.max(jnp.where(picked, m_all, -jnp.inf),
                                           axis=1, keepdims=True))
        w = jnp.where(picked, jnp.exp2(m_all - m_fin), 0.0)
        w_own = jnp.exp2(m_own - m_fin)
        l_fin = w_own * l_own + jnp.sum(w * l_scr[...], axis=1, keepdims=True)
        acc = w_own * acc_own
        for jb in range(n_blocks):
            acc = acc + w[:, jb:jb + 1] * acc_scr[jb]
        o = acc / l_fin
        for h in range(N_HEADS):
            o_ref[:, _head_cols(h)] = o[Q_ROWS * h:Q_ROWS * h + dec, :]


def _moba_dec(page_table, zs, cache, *, layer, dec, pages):
    n_seq, n_pages = page_table.shape
    page = cache.shape[2]
    past = n_pages * page
    assert dec == 8 and past % MOBA_BLOCK == 0 and MOBA_BLOCK % page == 0
    assert DEC_SPAN % (MOBA_BLOCK // page) == 0 and pages % DEC_SPAN == 0 and n_pages % pages == 0
    n_blocks = past // MOBA_BLOCK
    assert MOBA_TOPK <= n_blocks <= LANES
    kern = functools.partial(_moba_dec_kernel, pages=pages, page=page, n_blocks=n_blocks)
    row_specs, cache_specs = _dec_specs(dec, page, pages, layer, (T_QB, T_KB, T_VB))
    stat = pltpu.VMEM((N_HEADS * Q_ROWS, LANES), F32)
    return pl.pallas_call(
        kern,
        grid_spec=pltpu.PrefetchScalarGridSpec(
            num_scalar_prefetch=1,
            grid=(n_seq, n_pages // pages),
            in_specs=row_specs + cache_specs,
            out_specs=pl.BlockSpec((dec, BRANCH_W), lambda b, c, pt: (b, 0)),
            scratch_shapes=[pltpu.VMEM((N_HEADS * Q_ROWS, HEAD_W), BF16), stat, stat, stat,
                            pltpu.VMEM((n_blocks, N_HEADS * Q_ROWS, HEAD_W), F32)]),
        out_shape=jax.ShapeDtypeStruct((n_seq * dec, BRANCH_W), F32),
        compiler_params=pltpu.CompilerParams(
            dimension_semantics=("parallel", "arbitrary"), vmem_limit_bytes=VMEM_LIMIT),
        name="moba_dec",
    )(page_table, zs, zs, zs, *([cache] * pages))


def _out_kernel(x_ref, oa_ref, ob_ref, ga_ref, gb_ref, gm_ref, wa_ref, wb_ref, wo_ref, y_ref):
    d_model = x_ref.shape[1]
    ua = (oa_ref[...] * ga_ref[...].astype(F32)).astype(BF16)
    ub = (ob_ref[...] * gb_ref[...].astype(F32)).astype(BF16)
    ya = jnp.dot(ua, wa_ref[...], preferred_element_type=F32)
    yb = jnp.dot(ub, wb_ref[...], preferred_element_type=F32)
    merged = (gm_ref[:, :d_model].astype(F32) * ya + gm_ref[:, d_model:].astype(F32) * yb)
    y_ref[...] = x_ref[...] + jnp.dot(merged.astype(BF16), wo_ref[...],
                                      preferred_element_type=F32)


def _out(x, oa, ob, zb, wa, wb, wo, *, layer, tm):
    rows, d_model = x.shape
    assert 2 * d_model == 4 * BRANCH_W
    const = lambda shape: pl.BlockSpec((None,) + shape, lambda m: (layer, 0, 0),
                                       pipeline_mode=pl.Buffered(1))
    return pl.pallas_call(
        _out_kernel,
        grid=(rows // tm,),
        in_specs=[
            pl.BlockSpec((tm, d_model), lambda m: (m, 0)),
            pl.BlockSpec((tm, BRANCH_W), lambda m: (m, 0)),
            pl.BlockSpec((tm, BRANCH_W), lambda m: (m, 0)),
            pl.BlockSpec((tm, BRANCH_W), lambda m: (m, T_GA)),
            pl.BlockSpec((tm, BRANCH_W), lambda m: (m, T_GB)),
            pl.BlockSpec((tm, 2 * d_model), lambda m: (m, T_GM * BRANCH_W // (2 * d_model))),
            const((BRANCH_W, d_model)),
            const((BRANCH_W, d_model)),
            const((d_model, d_model)),
        ],
        out_specs=pl.BlockSpec((tm, d_model), lambda m: (m, 0)),
        out_shape=jax.ShapeDtypeStruct((rows, d_model), F32),
        compiler_params=pltpu.CompilerParams(
            dimension_semantics=("parallel",), vmem_limit_bytes=VMEM_LIMIT),
        name="out",
    )(x, oa, ob, zb, zb, zb, wa, wb, wo)


def _rope_table(pos):
    pos = pos.astype(F32)[:, None]
    parts = []
    for d in (D_A, HEAD_W):
        half = d // 2
        inv = ROPE_THETA ** (-2.0 * jnp.arange(half, dtype=F32) / d)
        ang = pos * inv[None, :]
        cos, sin = jnp.cos(ang), jnp.sin(ang)
        reps = LANES // d
        parts.append(jnp.tile(jnp.concatenate([cos, cos], axis=-1), (1, reps)))
        parts.append(jnp.tile(jnp.concatenate([-sin, sin], axis=-1), (1, reps)))
    return jnp.concatenate(parts, axis=-1)


def _aux_rows(q_norm_a, k_norm_a, q_norm_b, k_norm_b, b_merge):
    zero = jnp.zeros((BRANCH_W,), F32)
    rows = [jnp.tile(q_norm_a, BRANCH_W // D_A), jnp.tile(k_norm_a, BRANCH_W // D_A), zero, zero,
            jnp.tile(q_norm_b, BRANCH_W // HEAD_W), jnp.tile(k_norm_b, BRANCH_W // HEAD_W),
            zero, zero]
    rows += list(b_merge.reshape(-1, BRANCH_W))
    return jnp.stack(rows)[:, None, :]


def kernel(x_prompt, x_sample, cache_kv_diff, cache_kv_moba, page_table, norm_gain, w_in, q_norm_a, k_norm_a, lambda_q1, lambda_k1, lambda_q2, lambda_k2, subln_gain, q_norm_b, k_norm_b, b_merge, w_up_a, w_up_b, w_out):
    batch, seq, d_model = x_prompt.shape
    n_seq, dec, _ = x_sample.shape
    depth, n_pool, page = cache_kv_diff.shape[:3]
    n_pages = page_table.shape[1]
    past = n_pages * page
    assert w_in.shape[2] == N_TILES * BRANCH_W and seq % MOBA_BLOCK == 0

    cache_d, cache_m = cache_kv_diff, cache_kv_moba
    assert cache_d.shape[3:] == cache_m.shape[3:] == (2, N_HEADS, HEAD_W)
    tab_p = _rope_table(jnp.arange(seq))
    tab_s = jnp.tile(_rope_table(past + jnp.arange(dec)), (n_seq, 1))

    yp = x_prompt.reshape(batch * seq, d_model)
    ys = x_sample.reshape(n_seq * dec, d_model)
    kv_p = kv_s = None
    w_in_l = w_in.astype(BF16)
    wa, wb, wo = w_up_a.astype(BF16), w_up_b.astype(BF16), w_out.astype(BF16)
    for l in range(depth):
        lam_init = _lambda_init(l)
        g = norm_gain[l][None, :]
        aux = _aux_rows(q_norm_a[l], k_norm_a[l], q_norm_b[l], k_norm_b[l], b_merge[l])
        lam_params = jnp.stack([lambda_q1[l], lambda_k1[l], lambda_q2[l], lambda_k2[l]])
        sg = subln_gain[l][None, :]

        zb, *kv_p, vt = _proj(yp, g, w_in_l, tab_p, aux, kv_p, layer=l, depth=depth, tm=512,
                              z_dtype=BF16, vt_seq=seq)
        oa = _diff_attn(zb, vt, lam_params, sg, batch=batch, seq=seq, lam_init=lam_init)
        ob = _moba_attn(zb, vt, batch=batch, seq=seq)
        yp = _out(yp, oa, ob, zb, wa, wb, wo, layer=l, tm=256)

        zs, *kv_s = _proj(ys, g, w_in_l, tab_s, aux, kv_s, layer=l, depth=depth, tm=n_seq * dec,
                          z_dtype=F32)
        oa = _diff_dec(page_table, zs, cache_d, lam_params, sg, layer=l, dec=dec, pages=DEC_PAGES,
                       lam_init=lam_init)
        ob = _moba_dec(page_table, zs, cache_m, layer=l, dec=dec, pages=DEC_PAGES)
        ys = _out(ys, oa, ob, zs, wa, wb, wo, layer=l, tm=n_seq * dec)

    kv_shape = lambda rows_a, rows_b: (depth, rows_a, rows_b, 2, N_HEADS, HEAD_W)
    return (yp.reshape(batch, seq, d_model),
            ys.reshape(n_seq, dec, d_model),
            kv_p[0].reshape(kv_shape(batch, seq)),
            kv_p[1].reshape(kv_shape(batch, seq)),
            kv_s[0].reshape(kv_shape(n_seq, dec)),
            kv_s[1].reshape(kv_shape(n_seq, dec)))
```
